```python
import math
import jax, jax.numpy as jnp
from jax import lax
import numpy as np

D_MODEL = 1024
BATCH = 2
SEQ = 16384
DEPTH = 2

N_META = 16
N_MIXERS = 2
N_A_LAYERS = (DEPTH + 1) // 2
N_B_LAYERS = DEPTH // 2

DA_HEADS = 8
DA_HEAD_DIM = D_MODEL // (2 * DA_HEADS)
DA_V_DIM = 2 * DA_HEAD_DIM
Q_BLOCK = 128
SUBLN_EPS = 1e-5

RW_HEAD_DIM = 64
RW_HEADS = D_MODEL // RW_HEAD_DIM
RW_DECAY_LORA = 64
RW_AAA_LORA = 64
RW_GATE_LORA = 160
RW_N_MIX = 6
RW_LNX_EPS = 64e-5

D_FF = 2816
N_EXPERTS = 8
TOP_K = 2
D_FF_EXPERT = 3584

LN_EPS = 1e-5
DEEPNORM_ALPHA = (2.0 * DEPTH) ** 0.25
DEEPNORM_BETA = (8.0 * DEPTH) ** -0.25

kernel_name = 'hybrid_diffattn_rwkv7_moe_encoder'


def layer_norm(x, g, b):
    xf = x.astype(jnp.float32)
    mu = jnp.mean(xf, -1, keepdims=True)
    var = jnp.mean(jnp.square(xf - mu), -1, keepdims=True)
    return ((xf - mu) * lax.rsqrt(var + LN_EPS) * g + b).astype(x.dtype)


def diff_attention(x, w_in, w_o, lq1, lk1, lq2, lk2, subln_g, layer_idx):
    B, L, D = x.shape
    H, Dh, Dv = DA_HEADS, DA_HEAD_DIM, DA_V_DIM
    Lp = -(-L // Q_BLOCK) * Q_BLOCK
    nblk = Lp // Q_BLOCK
    qkv = jnp.einsum('bld,de->ble', x, w_in)
    qkv = jnp.pad(qkv, ((0, 0), (0, Lp - L), (0, 0)))
    q, k, v = jnp.split(qkv, 3, axis=-1)
    q = q.reshape(B, Lp, H, 2, Dh)
    k = k.reshape(B, Lp, H, 2, Dh)
    vf = v.reshape(B, Lp, H, Dv).astype(jnp.float32)
    lam_init = 0.8 - 0.6 * math.exp(-0.3 * layer_idx)
    lam = (jnp.exp(jnp.sum(lq1 * lk1).astype(jnp.float32))
           - jnp.exp(jnp.sum(lq2 * lk2).astype(jnp.float32)) + lam_init)
    slopes = 2.0 ** (-(8.0 / H) * jnp.arange(1, H + 1, dtype=jnp.float32))
    kpos = jnp.arange(Lp, dtype=jnp.int32)
    kvalid = kpos < L
    scale = Dh ** -0.5
    qb = q.reshape(B, nblk, Q_BLOCK, H, 2, Dh).transpose(1, 0, 2, 3, 4, 5)
    starts = jnp.arange(nblk, dtype=jnp.int32) * Q_BLOCK

    def block(args):
        qblk, start = args
        s = jnp.einsum('bqhcd,bkhcd->bhcqk', qblk, k).astype(jnp.float32) * scale
        qpos = start + jnp.arange(Q_BLOCK, dtype=jnp.int32)
        dist = jnp.abs(qpos[:, None] - kpos[None, :]).astype(jnp.float32)
        s = s - slopes[None, :, None, None, None] * dist
        s = jnp.where(kvalid, s, -jnp.inf)
        p = jax.nn.softmax(s, axis=-1)
        a = p[:, :, 0] - lam * p[:, :, 1]
        return jnp.einsum('bhqk,bkhe->bqhe', a, vf)

    o = lax.map(block, (qb, starts))
    o = o.transpose(1, 0, 2, 3, 4).reshape(B, Lp, H, Dv)[:, :L]
    o = o * lax.rsqrt(jnp.mean(o * o, -1, keepdims=True) + SUBLN_EPS) * subln_g * (1.0 - lam_init)
    o = o.reshape(B, L, H * Dv).astype(x.dtype)
    return jnp.einsum('ble,ed->bld', o, w_o)


def _wkv7_step(S, inp):
    r, w, k, v, kk, a = inp
    sa = jnp.einsum('bhvk,bhk->bhv', S, kk)
    S = S * w[:, :, None, :] - sa[..., None] * (kk * a)[:, :, None, :] + v[..., None] * k[:, :, None, :]
    y = jnp.einsum('bhvk,bhk->bhv', S, r)
    return S, y


def rwkv7_time_mix(x, mu, w_rkv, w0, w1, w2, a0, a1, a2, g1, g2, k_k, k_a, r_k, lnx_g, lnx_b, w_o):
    B, L, D = x.shape
    H, N = RW_HEADS, RW_HEAD_DIM
    zero = jnp.zeros_like(x[:, :1])
    dx_prev = jnp.concatenate([zero, x[:, :-1]], 1) - x
    dx_next = jnp.concatenate([x[:, 1:], zero], 1) - x
    xs = x[None] + mu[0][:, None, None, :] * dx_prev[None] + mu[1][:, None, None, :] * dx_next[None]
    r, k, v = jnp.einsum('ibld,ide->ible', xs[:3], w_rkv)
    lw = jnp.einsum('nblr,nrd->nbld', jnp.tanh(jnp.einsum('bld,ndr->nblr', xs[3], w1)), w2)
    w_log = -jax.nn.softplus(-(w0[:, None, None, :] + lw).astype(jnp.float32)) - 0.5
    decay = jnp.exp(-jnp.exp(w_log))
    la = jnp.einsum('nblr,nrd->nbld', jnp.einsum('bld,ndr->nblr', xs[4], a1), a2)
    a = jax.nn.sigmoid((a0[:, None, None, :] + la).astype(jnp.float32))
    g = jnp.einsum('blr,rd->bld', jax.nn.sigmoid(jnp.einsum('bld,dr->blr', xs[5], g1)), g2)
    kf = k.astype(jnp.float32)
    kk = (kf * k_k).reshape(B, L, H, N)
    kk = kk / jnp.maximum(jnp.sqrt(jnp.sum(kk * kk, -1, keepdims=True)), 1e-12)
    kd = kf[None] * (1.0 + (a - 1.0) * k_a)

    def tm(t):
        return jnp.swapaxes(t.reshape(B, L, H, N), 0, 1).astype(jnp.float32)

    r_t, v_t, kk_t = tm(r), tm(v), jnp.swapaxes(kk, 0, 1)
    S0 = jnp.zeros((B, H, N, N), jnp.float32)
    _, y_fwd = lax.scan(_wkv7_step, S0, (r_t, tm(decay[0]), tm(kd[0]), v_t, kk_t, tm(a[0])))
    _, y_bwd = lax.scan(_wkv7_step, S0, (r_t, tm(decay[1]), tm(kd[1]), v_t, kk_t, tm(a[1])), reverse=True)
    y = jnp.swapaxes(y_fwd + y_bwd, 0, 1)
    ym = jnp.mean(y, -1, keepdims=True)
    yv = jnp.mean(jnp.square(y - ym), -1, keepdims=True)
    y = ((y - ym) * lax.rsqrt(yv + RW_LNX_EPS)).reshape(B, L, D) * lnx_g + lnx_b
    rh = r.reshape(B, L, H, N).astype(jnp.float32)
    kdh = (kd[0] + kd[1]).reshape(B, L, H, N)
    bonus = jnp.sum(rh * kdh * r_k, -1, keepdims=True) * v.reshape(B, L, H, N).astype(jnp.float32)
    out = ((y + bonus.reshape(B, L, D)) * g).astype(x.dtype)
    return jnp.einsum('bld,de->ble', out, w_o)


def swiglu(x, w_gate, w_up, w_down):
    return (jax.nn.silu(x @ w_gate) * (x @ w_up)) @ w_down


def moe_swiglu(x, w_router, b_router, w_gate, w_up, w_down):
    logits = (jnp.einsum('bld,de->ble', x, w_router) + b_router).astype(jnp.float32)
    probs = jax.nn.softmax(logits, -1)
    top_p, top_i = lax.top_k(probs, TOP_K)
    top_p = top_p / jnp.sum(top_p, -1, keepdims=True)
    gates = jnp.sum(jax.nn.one_hot(top_i, N_EXPERTS, dtype=jnp.float32) * top_p[..., None], -2)
    y = jnp.zeros(x.shape, jnp.float32)
    for e in range(N_EXPERTS):
        y = y + gates[..., e:e + 1] * swiglu(x, w_gate[e], w_up[e], w_down[e]).astype(jnp.float32)
    return y.astype(x.dtype)


def setup_inputs(seed: int = 0) -> dict:
    key = jax.random.key(seed)
    ks = iter(jax.random.split(key, 48))
    f32 = jnp.float32
    D, F, FE, E = D_MODEL, D_FF, D_FF_EXPERT, N_EXPERTS
    nA, nB = N_A_LAYERS, N_B_LAYERS

    def nrm(shape, scale):
        return jax.random.normal(next(ks), shape, f32) * scale

    inp = {}
    inp['x'] = nrm((BATCH, SEQ, D), 1.0)
    inp['meta'] = nrm((N_META, D), 1.0)
    inp['ln_g'] = 1.0 + nrm((DEPTH, 2, D), 0.02)
    inp['ln_b'] = nrm((DEPTH, 2, D), 0.02)
    inp['attn_w_in'] = nrm((nA, D, 3 * D), D ** -0.5)
    inp['attn_w_o'] = nrm((nA, D, D), D ** -0.5 * DEEPNORM_BETA)
    inp['attn_lam_q1'] = nrm((nA, DA_HEAD_DIM), 0.1)
    inp['attn_lam_k1'] = nrm((nA, DA_HEAD_DIM), 0.1)
    inp['attn_lam_q2'] = nrm((nA, DA_HEAD_DIM), 0.1)
    inp['attn_lam_k2'] = nrm((nA, DA_HEAD_DIM), 0.1)
    inp['attn_subln_g'] = 1.0 + nrm((nA, DA_V_DIM), 0.02)
    inp['ffn_w_gate'] = nrm((nA, D, F), D ** -0.5)
    inp['ffn_w_up'] = nrm((nA, D, F), D ** -0.5)
    inp['ffn_w_down'] = nrm((nA, F, D), F ** -0.5 * DEEPNORM_BETA)
    inp['rw_mu'] = jax.random.uniform(next(ks), (nB, 2, RW_N_MIX, D), f32, 0.0, 0.5)
    inp['rw_w_rkv'] = nrm((nB, 3, D, D), D ** -0.5)
    inp['rw_w0'] = -1.0 + nrm((nB, 2, D), 0.5)
    inp['rw_w1'] = nrm((nB, 2, D, RW_DECAY_LORA), D ** -0.5)
    inp['rw_w2'] = nrm((nB, 2, RW_DECAY_LORA, D), 0.5 * RW_DECAY_LORA ** -0.5)
    inp['rw_a0'] = nrm((nB, 2, D), 0.1)
    inp['rw_a1'] = nrm((nB, 2, D, RW_AAA_LORA), D ** -0.5)
    inp['rw_a2'] = nrm((nB, 2, RW_AAA_LORA, D), 0.5 * RW_AAA_LORA ** -0.5)
    inp['rw_g1'] = nrm((nB, D, RW_GATE_LORA), D ** -0.5)
    inp['rw_g2'] = nrm((nB, RW_GATE_LORA, D), RW_GATE_LORA ** -0.5)
    inp['rw_k_k'] = 0.85 + nrm((nB, D), 0.05)
    inp['rw_k_a'] = 1.0 + nrm((nB, D), 0.05)
    inp['rw_r_k'] = nrm((nB, RW_HEADS, RW_HEAD_DIM), 0.1)
    inp['rw_lnx_g'] = 1.0 + nrm((nB, D), 0.02)
    inp['rw_lnx_b'] = nrm((nB, D), 0.02)
    inp['rw_w_o'] = nrm((nB, D, D), D ** -0.5 * DEEPNORM_BETA)
    inp['moe_w_router'] = nrm((nB, D, E), D ** -0.5)
    inp['moe_b_router'] = nrm((nB, E), 0.01)
    inp['moe_w_gate'] = nrm((nB, E, D, FE), D ** -0.5)
    inp['moe_w_up'] = nrm((nB, E, D, FE), D ** -0.5)
    inp['moe_w_down'] = nrm((nB, E, FE, D), FE ** -0.5 * DEEPNORM_BETA)
    return inp


def reference(x, meta, ln_g, ln_b, attn_w_in, attn_w_o, attn_lam_q1, attn_lam_k1, attn_lam_q2,
              attn_lam_k2, attn_subln_g, ffn_w_gate, ffn_w_up, ffn_w_down, rw_mu, rw_w_rkv, rw_w0,
              rw_w1, rw_w2, rw_a0, rw_a1, rw_a2, rw_g1, rw_g2, rw_k_k, rw_k_a, rw_r_k, rw_lnx_g,
              rw_lnx_b, rw_w_o, moe_w_router, moe_b_router, moe_w_gate, moe_w_up, moe_w_down):
    B = x.shape[0]
    h = jnp.concatenate([jnp.broadcast_to(meta[None].astype(x.dtype), (B, N_META, D_MODEL)), x], 1)
    for i in range(DEPTH):
        j = i // N_MIXERS
        if i % N_MIXERS == 0:
            mix = diff_attention(h, attn_w_in[j], attn_w_o[j], attn_lam_q1[j], attn_lam_k1[j],
                                 attn_lam_q2[j], attn_lam_k2[j], attn_subln_g[j], i)
        else:
            mix = rwkv7_time_mix(h, rw_mu[j], rw_w_rkv[j], rw_w0[j], rw_w1[j], rw_w2[j], rw_a0[j],
                                 rw_a1[j], rw_a2[j], rw_g1[j], rw_g2[j], rw_k_k[j], rw_k_a[j],
                                 rw_r_k[j], rw_lnx_g[j], rw_lnx_b[j], rw_w_o[j])
        h = layer_norm(DEEPNORM_ALPHA * h + mix, ln_g[i, 0], ln_b[i, 0])
        if i % 2 == 0:
            ff = swiglu(h, ffn_w_gate[j], ffn_w_up[j], ffn_w_down[j])
        else:
            ff = moe_swiglu(h, moe_w_router[j], moe_b_router[j], moe_w_gate[j], moe_w_up[j], moe_w_down[j])
        h = layer_norm(DEEPNORM_ALPHA * h + ff, ln_g[i, 1], ln_b[i, 1])
    return h[:, N_META:]
```

```python
import functools
import math

import jax
import jax.numpy as jnp
from jax import lax
from jax.experimental import pallas as pl
from jax.experimental.pallas import tpu as pltpu

F32 = jnp.float32
BF16 = jnp.bfloat16

N_META = 16
DEPTH = 2
LN_EPS = 1e-5
DEEPNORM_ALPHA = (2.0 * DEPTH) ** 0.25

DA_HEADS = 8
DA_HEAD_DIM = 64
DA_V_DIM = 128
SUBLN_EPS = 1e-5

RW_HEAD_DIM = 64
RW_LNX_EPS = 64e-5
RW_CHUNK = 64
RW_GROUP = 256

N_EXPERTS = 8
LANES = 128

SEQ_MULTIPLE = 1280
VMEM_LIMIT = 56 * 1024 * 1024
NEG_BIG = -1e30


def _cparams(*sem):
    return pltpu.CompilerParams(dimension_semantics=sem, vmem_limit_bytes=VMEM_LIMIT)


def _dot(a, b):
    return jnp.dot(a, b, preferred_element_type=F32)


def _dot_nt(a, b):
    return lax.dot_general(a, b, (((1,), (1,)), ((), ())), preferred_element_type=F32)


def _dot_tn(a, b):
    return lax.dot_general(a, b, (((0,), (0,)), ((), ())), preferred_element_type=F32)


def _split2(x):
    hi = x.astype(BF16)
    lo = (x - hi.astype(F32)).astype(BF16)
    return hi, lo


def _layer_norm_rows(z, g, b):
    mu = jnp.mean(z, -1, keepdims=True)
    zc = z - mu
    var = jnp.mean(zc * zc, -1, keepdims=True)
    return zc * lax.rsqrt(var + LN_EPS) * g + b


def _mm_kernel(a_ref, w_ref, o_ref):
    o_ref[...] = _dot(a_ref[...].astype(BF16), w_ref[...]).astype(o_ref.dtype)


def matmul(a, w, out_dtype, bm, bn):
    M, K = a.shape
    N = w.shape[1]
    return pl.pallas_call(
        _mm_kernel,
        grid=(N // bn, M // bm),
        in_specs=[pl.BlockSpec((bm, K), lambda j, i: (i, 0)),
                  pl.BlockSpec((K, bn), lambda j, i: (0, j))],
        out_specs=pl.BlockSpec((bm, bn), lambda j, i: (i, j)),
        out_shape=jax.ShapeDtypeStruct((M, N), out_dtype),
        compiler_params=_cparams("parallel", "parallel"),
    )(a, w)


def _mm_res_ln_kernel(a_ref, w_ref, res_ref, g_ref, b_ref, o_ref, ob_ref, acc_ref):
    k = pl.program_id(1)

    @pl.when(k == 0)
    def _():
        acc_ref[...] = jnp.zeros_like(acc_ref)

    acc_ref[...] += _dot(a_ref[...].astype(BF16), w_ref[...])

    @pl.when(k == pl.num_programs(1) - 1)
    def _():
        z = DEEPNORM_ALPHA * res_ref[...] + acc_ref[...]
        o = _layer_norm_rows(z, g_ref[...], b_ref[...])
        o_ref[...] = o
        ob_ref[...] = o.astype(BF16)


def matmul_res_ln(a, w, res, g, b, bm, bk):
    M, K = a.shape
    N = w.shape[1]
    return pl.pallas_call(
        _mm_res_ln_kernel,
        grid=(M // bm, K // bk),
        in_specs=[pl.BlockSpec((bm, bk), lambda i, k: (i, k)),
                  pl.BlockSpec((bk, N), lambda i, k: (k, 0)),
                  pl.BlockSpec((bm, N), lambda i, k: (i, 0)),
                  pl.BlockSpec((1, N), lambda i, k: (0, 0)),
                  pl.BlockSpec((1, N), lambda i, k: (0, 0))],
        out_specs=[pl.BlockSpec((bm, N), lambda i, k: (i, 0)),
                   pl.BlockSpec((bm, N), lambda i, k: (i, 0))],
        out_shape=[jax.ShapeDtypeStruct((M, N), F32), jax.ShapeDtypeStruct((M, N), BF16)],
        scratch_shapes=[pltpu.VMEM((bm, N), F32)],
        compiler_params=_cparams("parallel", "arbitrary"),
    )(a, w, res, g.reshape(1, N), b.reshape(1, N))


def _ffn_up_kernel(x_ref, wg_ref, wu_ref, gates_ref, o_ref, *, nj, use_gates):
    x = x_ref[...]
    g = _dot(x, wg_ref[...])
    u = _dot(x, wu_ref[...])
    act = g * jax.nn.sigmoid(g) * u
    if use_gates:
        e = pl.program_id(0) // nj
        gates = gates_ref[...]
        lane = lax.broadcasted_iota(jnp.int32, gates.shape, 1)
        gcol = jnp.sum(jnp.where(lane == e, gates, 0.0), axis=-1, keepdims=True)
        act = act * gcol
    o_ref[...] = act.astype(o_ref.dtype)


def ffn_up(xb, wg, wu, gates, bm, bn):
    M, D = xb.shape
    E, _, Fe = wg.shape
    nj = Fe // bn
    use_gates = gates is not None
    if gates is None:
        gates = jnp.zeros((M, LANES), F32)
    return pl.pallas_call(
        functools.partial(_ffn_up_kernel, nj=nj, use_gates=use_gates),
        grid=(E * nj, M // bm),
        in_specs=[pl.BlockSpec((bm, D), lambda n, i: (i, 0)),
                  pl.BlockSpec((None, D, bn), lambda n, i: (n // nj, 0, n % nj)),
                  pl.BlockSpec((None, D, bn), lambda n, i: (n // nj, 0, n % nj)),
                  pl.BlockSpec((bm, LANES), lambda n, i: (i, 0))],
        out_specs=pl.BlockSpec((bm, bn), lambda n, i: (i, n)),
        out_shape=jax.ShapeDtypeStruct((M, E * Fe), BF16),
        compiler_params=_cparams("parallel", "parallel"),
    )(xb, wg, wu, gates)


def _attn_kernel(slopes_ref, q_ref, k_ref, v_ref, lq1_ref, lk1_ref, lq2_ref, lk2_ref, sg_ref,
                 o_ref, *, bq, bk, nk, seq_len, lam_init):
    h = pl.program_id(1)
    qi = pl.program_id(2)
    slope = slopes_ref[h]
    scale = DA_HEAD_DIM ** -0.5

    q = q_ref[...].astype(F32) * scale
    lane = lax.broadcasted_iota(jnp.int32, q.shape, 1)
    q1 = jnp.where(lane < DA_HEAD_DIM, q, 0.0).astype(BF16)
    q2 = jnp.where(lane >= DA_HEAD_DIM, q, 0.0).astype(BF16)
    qpos = qi * bq + lax.broadcasted_iota(jnp.int32, (bq, 1), 0)

    def online(s, m, l, acc, vb):
        m_new = jnp.maximum(m, jnp.max(s, axis=-1, keepdims=True))
        alpha = jnp.exp(m - m_new)
        p = jnp.exp(s - m_new)
        l_new = alpha * l + jnp.sum(p, axis=-1, keepdims=True)
        acc_new = alpha * acc + _dot(p.astype(BF16), vb)
        return m_new, l_new, acc_new

    def body(j, carry):
        m1, l1, a1, m2, l2, a2 = carry
        off = pl.multiple_of(j * bk, bk)
        kb = k_ref[pl.ds(off, bk), :]
        vb = v_ref[pl.ds(off, bk), :]
        kpos = off + lax.broadcasted_iota(jnp.int32, (1, bk), 1)
        bias = -slope * jnp.abs(qpos - kpos).astype(F32)
        bias = jnp.where(kpos < seq_len, bias, NEG_BIG)
        s1 = _dot_nt(q1, kb) + bias
        s2 = _dot_nt(q2, kb) + bias
        m1, l1, a1 = online(s1, m1, l1, a1, vb)
        m2, l2, a2 = online(s2, m2, l2, a2, vb)
        return m1, l1, a1, m2, l2, a2

    m0 = jnp.full((bq, 1), NEG_BIG, F32)
    l0 = jnp.zeros((bq, 1), F32)
    a0 = jnp.zeros((bq, DA_V_DIM), F32)
    m1, l1, a1, m2, l2, a2 = lax.fori_loop(0, nk, body, (m0, l0, a0, m0, l0, a0))

    lam = (jnp.exp(jnp.sum(lq1_ref[...] * lk1_ref[...], axis=-1, keepdims=True))
           - jnp.exp(jnp.sum(lq2_ref[...] * lk2_ref[...], axis=-1, keepdims=True)) + lam_init)
    o = a1 / l1 - lam * (a2 / l2)
    o = o * lax.rsqrt(jnp.mean(o * o, -1, keepdims=True) + SUBLN_EPS) * sg_ref[...] * (1.0 - lam_init)
    o_ref[...] = o.astype(o_ref.dtype)


def diff_attention_core(qkv, lq1, lk1, lq2, lk2, subln_g, B, Lp, seq_len, layer_idx, bq, bk):
    Tp, D3 = qkv.shape
    D = D3 // 3
    H = DA_HEADS
    nq = Lp // bq
    nk = Lp // bk
    lam_init = 0.8 - 0.6 * math.exp(-0.3 * layer_idx)
    slopes = jnp.asarray([2.0 ** (-(8.0 / H) * (i + 1)) for i in range(H)], F32)
    hd = DA_HEAD_DIM
    grid_spec = pltpu.PrefetchScalarGridSpec(
        num_scalar_prefetch=1,
        grid=(B, H, nq),
        in_specs=[pl.BlockSpec((bq, 2 * hd), lambda b, h, i, s: (b * nq + i, h)),
                  pl.BlockSpec((Lp, 2 * hd), lambda b, h, i, s: (b, H + h)),
                  pl.BlockSpec((Lp, DA_V_DIM), lambda b, h, i, s: (b, 2 * H + h)),
                  pl.BlockSpec((1, hd), lambda b, h, i, s: (0, 0)),
                  pl.BlockSpec((1, hd), lambda b, h, i, s: (0, 0)),
                  pl.BlockSpec((1, hd), lambda b, h, i, s: (0, 0)),
                  pl.BlockSpec((1, hd), lambda b, h, i, s: (0, 0)),
                  pl.BlockSpec((1, DA_V_DIM), lambda b, h, i, s: (0, 0))],
        out_specs=pl.BlockSpec((bq, DA_V_DIM), lambda b, h, i, s: (b * nq + i, h)),
    )
    return pl.pallas_call(
        functools.partial(_attn_kernel, bq=bq, bk=bk, nk=nk, seq_len=seq_len, lam_init=lam_init),
        grid_spec=grid_spec,
        out_shape=jax.ShapeDtypeStruct((Tp, D), BF16),
        compiler_params=_cparams("parallel", "parallel", "arbitrary"),
    )(slopes, qkv, qkv, qkv, lq1.reshape(1, hd), lk1.reshape(1, hd), lq2.reshape(1, hd),
      lk2.reshape(1, hd), subln_g.reshape(1, DA_V_DIM))


def _seg_sum(x, ones_bd):
    hi, lo = _split2(x)
    return _dot(hi, ones_bd) + _dot(lo, ones_bd)


def _rw_prep_kernel(x_ref, xp_ref, xn_ref, mu_ref, wrkv_ref, w1_ref, w2_ref, a1_ref, a2_ref,
                    g1_ref, g2_ref, vec_ref, ones_ref,
                    r_ref, v_ref, kk_ref, g_ref, bonus_ref, lw_ref, kd_ref, b_ref,
                    *, bm, Lp, seq_len, lora_w, lora_a):
    i = pl.program_id(0)
    x = x_ref[...]
    row = lax.broadcasted_iota(jnp.int32, (bm, 1), 0)
    pos = (i * bm) % Lp + row
    prev_last = xp_ref[7:8, :]
    next_first = xn_ref[0:1, :]
    x_prev = jnp.where(row == 0, prev_last, pltpu.roll(x, 1, 0))
    x_next = jnp.where(row == bm - 1, next_first, pltpu.roll(x, bm - 1, 0))
    x_prev = jnp.where(pos == 0, 0.0, x_prev)
    x_next = jnp.where(pos == seq_len - 1, 0.0, x_next)
    dxp = x_prev - x
    dxn = x_next - x
    valid = pos < seq_len

    def mix(n):
        return (x + mu_ref[n:n + 1, :] * dxp + mu_ref[6 + n:7 + n, :] * dxn).astype(BF16)

    r = _dot(mix(0), wrkv_ref[0])
    k = _dot(mix(1), wrkv_ref[1])
    v = _dot(mix(2), wrkv_ref[2])

    lane = lax.broadcasted_iota(jnp.int32, (bm, 2 * lora_w), 1)
    th = jnp.tanh(_dot(mix(3), w1_ref[...]))
    lw = [_dot(jnp.where((lane >= n * lora_w) & (lane < (n + 1) * lora_w), th, 0.0).astype(BF16),
               w2_ref[...]) for n in range(2)]
    lane = lax.broadcasted_iota(jnp.int32, (bm, 2 * lora_a), 1)
    ah = _dot(mix(4), a1_ref[...])
    la = [_dot(jnp.where((lane >= n * lora_a) & (lane < (n + 1) * lora_a), ah, 0.0).astype(BF16),
               a2_ref[...]) for n in range(2)]
    g = _dot(jax.nn.sigmoid(_dot(mix(5), g1_ref[...])).astype(BF16), g2_ref[...])

    k_k = vec_ref[4:5, :]
    k_a = vec_ref[5:6, :]
    r_k = vec_ref[6:7, :]
    ones_bd = ones_ref[...]

    kkr = k * k_k
    nrm = jnp.sqrt(_seg_sum(kkr * kkr, ones_bd))
    kk = kkr / jnp.maximum(nrm, 1e-12)
    kk = jnp.where(valid, kk, 0.0)
    vz = jnp.where(valid, v, 0.0)

    kd_sum = jnp.zeros_like(k)
    for n in range(2):
        z = -(vec_ref[n:n + 1, :] + lw[n])
        softplus = jnp.maximum(z, 0.0) + jnp.log(1.0 + jnp.exp(-jnp.abs(z)))
        w_log = -softplus - 0.5
        lw_ref[n] = jnp.where(valid, -jnp.exp(w_log), 0.0)
        a = jax.nn.sigmoid(vec_ref[2 + n:3 + n, :] + la[n])
        kd = k * (1.0 + (a - 1.0) * k_a)
        kd_sum = kd_sum + kd
        kd_ref[n] = jnp.where(valid, kd, 0.0)
        b_ref[n] = kk * a

    bonus = _seg_sum(r * kd_sum * r_k, ones_bd) * v
    r_ref[...] = r
    v_ref[...] = vz
    kk_ref[...] = kk
    g_ref[...] = g
    bonus_ref[...] = bonus


def rwkv_prep(h, mu, w_rkv, w1, w2, a1, a2, g1, g2, w0, a0, k_k, k_a, r_k, Lp, seq_len, bm):
    Tp, D = h.shape
    lora_w = w1.shape[-1]
    lora_a = a1.shape[-1]
    lora_g = g1.shape[-1]
    gpad = -(-lora_g // LANES) * LANES
    mu12 = mu.reshape(12, D)
    w1c = jnp.concatenate([w1[0], w1[1]], axis=1).astype(BF16)
    w2c = jnp.concatenate([w2[0], w2[1]], axis=0).astype(BF16)
    a1c = jnp.concatenate([a1[0], a1[1]], axis=1).astype(BF16)
    a2c = jnp.concatenate([a2[0], a2[1]], axis=0).astype(BF16)
    g1p = jnp.pad(g1, ((0, 0), (0, gpad - lora_g))).astype(BF16)
    g2p = jnp.pad(g2, ((0, gpad - lora_g), (0, 0))).astype(BF16)
    vecs = jnp.concatenate([w0, a0, k_k.reshape(1, D), k_a.reshape(1, D), r_k.reshape(1, D),
                            jnp.zeros((1, D), F32)], axis=0)
    seg = jnp.arange(D) // RW_HEAD_DIM
    ones_bd = (seg[:, None] == seg[None, :]).astype(BF16)
    nb8 = bm // 8
    last8 = Tp // 8 - 1
    const2 = lambda i: (0, 0)
    const3 = lambda i: (0, 0, 0)
    row = pl.BlockSpec((bm, D), lambda i: (i, 0))
    row2 = pl.BlockSpec((2, bm, D), lambda i: (0, i, 0))
    sds = jax.ShapeDtypeStruct((Tp, D), F32)
    sds2 = jax.ShapeDtypeStruct((2, Tp, D), F32)
    return pl.pallas_call(
        functools.partial(_rw_prep_kernel, bm=bm, Lp=Lp, seq_len=seq_len, lora_w=lora_w, lora_a=lora_a),
        grid=(Tp // bm,),
        in_specs=[row,
                  pl.BlockSpec((8, D), lambda i: (jnp.maximum(i * nb8 - 1, 0), 0)),
                  pl.BlockSpec((8, D), lambda i: (jnp.minimum((i + 1) * nb8, last8), 0)),
                  pl.BlockSpec((12, D), const2),
                  pl.BlockSpec((3, D, D), const3),
                  pl.BlockSpec((D, 2 * lora_w), const2),
                  pl.BlockSpec((2 * lora_w, D), const2),
                  pl.BlockSpec((D, 2 * lora_a), const2),
                  pl.BlockSpec((2 * lora_a, D), const2),
                  pl.BlockSpec((D, gpad), const2),
                  pl.BlockSpec((gpad, D), const2),
                  pl.BlockSpec((8, D), const2),
                  pl.BlockSpec((D, D), const2)],
        out_specs=[row, row, row, row, row, row2, row2, row2],
        out_shape=[sds, sds, sds, sds, sds, sds2, sds2, sds2],
        compiler_params=_cparams("parallel"),
    )(h, h, h, mu12, w_rkv.astype(BF16), w1c, w2c, a1c, a2c, g1p, g2p, vecs, ones_bd)


def _rw_scan_kernel(r_ref, v_ref, kk_ref, lw_ref, kd_ref, b_ref, y_ref, s_ref, *, C, D):
    d = pl.program_id(1)
    c = pl.program_id(2)
    G_ = RW_GROUP
    hpg = G_ // C
    fwd = d == 0

    @pl.when(c == 0)
    def _():
        s_ref[...] = jnp.zeros_like(s_ref)

    sgn = jnp.where(fwd, 1, -1)
    ri = lax.broadcasted_iota(jnp.int32, (C, C), 0)
    ci = lax.broadcasted_iota(jnp.int32, (C, C), 1)
    tri = jnp.where((ri - ci) * sgn >= 0, 1.0, 0.0).astype(BF16)

    lw = lw_ref[...]
    hi = lw.astype(BF16)
    r1 = lw - hi.astype(F32)
    mid = r1.astype(BF16)
    lo = (r1 - mid.astype(F32)).astype(BF16)
    G = _dot(tri, hi) + _dot(tri, mid) + _dot(tri, lo)
    g_end = jnp.where(fwd, G[C - 1:C, :], G[0:1, :])
    e_pos = jnp.exp(G)
    e_neg = jnp.exp(-G)
    e_exc = jnp.exp(G - lw)
    e_rem = jnp.exp(g_end - G)
    kk = kk_ref[...]
    b = b_ref[...]
    kd = kd_ref[...]
    At = (-kk * e_exc).astype(BF16)
    Bt = (b * e_neg).astype(BF16)
    Kt = (kd * e_neg).astype(BF16)
    Rt = (r_ref[...] * e_pos).astype(BF16)
    Bh = (b * e_rem).astype(BF16)
    Kh = (kd * e_rem).astype(BF16)
    Vb = v_ref[...].astype(BF16)
    gam = jnp.exp(g_end)

    R = lax.broadcasted_iota(jnp.int32, (G_, G_), 0)
    Cc = lax.broadcasted_iota(jnp.int32, (G_, G_), 1)
    blk = (R // C) == (Cc // RW_HEAD_DIM)
    dt = jnp.where((R // C) == (Cc // C), (R % C - Cc % C) * sgn, -1)
    strict = dt > 0
    incl = dt >= 0
    eye = jnp.where(R == Cc, 1.0, 0.0)
    zero_b = jnp.zeros((G_, G_), BF16)

    def expand(x, q):
        xq = x[:, q * G_:(q + 1) * G_]
        return jnp.where(blk, jnp.concatenate([xq] * hpg, axis=0), zero_b)

    def mm3(x, y):
        xh, xl = _split2(x)
        yh, yl = _split2(y)
        return _dot(xh, yh) + _dot(xh, yl) + _dot(xl, yh)

    for q in range(D // G_):
        Ae, Be, Ke, Re = expand(At, q), expand(Bt, q), expand(Kt, q), expand(Rt, q)
        Bhe, Khe, Ve = expand(Bh, q), expand(Kh, q), expand(Vb, q)
        S = s_ref[q]
        Sb = S.astype(BF16)
        N = jnp.where(strict, _dot_nt(Ae, Be), 0.0)
        Aak = jnp.where(strict, _dot_nt(Ae, Ke), 0.0).astype(BF16)
        Arb = jnp.where(incl, _dot_nt(Re, Be), 0.0).astype(BF16)
        Ark = jnp.where(incl, _dot_nt(Re, Ke), 0.0).astype(BF16)
        P = eye + N
        X = N
        p = 1
        while 2 * p < C:
            X = mm3(X, X)
            P = P + mm3(P, X)
            p *= 2
        rhs = _dot_nt(Ae, Sb) + _dot(Aak, Ve)
        U = mm3(P, rhs)
        Ub = U.astype(BF16)
        Y = _dot_nt(Re, Sb) + _dot(Arb, Ub) + _dot(Ark, Ve)
        yq = Y[0:C, :]
        for hb in range(1, hpg):
            yq = yq + Y[hb * C:(hb + 1) * C, :]
        y_ref[:, q * G_:(q + 1) * G_] = yq
        s_ref[q] = S * gam[:, q * G_:(q + 1) * G_] + _dot_tn(Ub, Bhe) + _dot_tn(Ve, Khe)


def rwkv_scan(r, v, kk, lw, kd, b, B, Lp):
    Tp, D = r.shape
    C = RW_CHUNK
    nch = Lp // C

    def rb(bi, d, c):
        return bi * nch + jnp.where(d == 0, c, nch - 1 - c)

    shared = pl.BlockSpec((C, D), lambda bi, d, c: (rb(bi, d, c), 0))
    perdir = pl.BlockSpec((None, C, D), lambda bi, d, c: (d, rb(bi, d, c), 0))
    return pl.pallas_call(
        functools.partial(_rw_scan_kernel, C=C, D=D),
        grid=(B, 2, nch),
        in_specs=[shared, shared, shared, perdir, perdir, perdir],
        out_specs=perdir,
        out_shape=jax.ShapeDtypeStruct((2, Tp, D), F32),
        scratch_shapes=[pltpu.VMEM((D // RW_GROUP, RW_GROUP, RW_GROUP), F32)],
        compiler_params=_cparams("parallel", "parallel", "arbitrary"),
    )(r, v, kk, lw, kd, b)


def _rw_post_kernel(yf_ref, yb_ref, g_ref, bonus_ref, res_ref, vec_ref, avg_ref, wo_ref,
                    o_ref, ob_ref):
    avg = avg_ref[...]
    y = yf_ref[...] + yb_ref[...]
    ym = _seg_sum(y, avg)
    yc = y - ym
    yv = _seg_sum(yc * yc, avg)
    yn = yc * lax.rsqrt(yv + RW_LNX_EPS) * vec_ref[0:1, :] + vec_ref[1:2, :]
    out = ((yn + bonus_ref[...]) * g_ref[...]).astype(BF16)
    z = DEEPNORM_ALPHA * res_ref[...] + _dot(out, wo_ref[...])
    o = _layer_norm_rows(z, vec_ref[2:3, :], vec_ref[3:4, :])
    o_ref[...] = o
    ob_ref[...] = o.astype(BF16)


def rwkv_post(y2, g, bonus, res, lnx_g, lnx_b, w_o, ln_g, ln_b, bm):
    Tp, D = res.shape
    vecs = jnp.concatenate([lnx_g.reshape(1, D), lnx_b.reshape(1, D), ln_g.reshape(1, D),
                            ln_b.reshape(1, D), jnp.zeros((4, D), F32)], axis=0)
    seg = jnp.arange(D) // RW_HEAD_DIM
    avg = ((seg[:, None] == seg[None, :]).astype(F32) / RW_HEAD_DIM).astype(BF16)
    row = pl.BlockSpec((bm, D), lambda i: (i, 0))
    return pl.pallas_call(
        _rw_post_kernel,
        grid=(Tp // bm,),
        in_specs=[pl.BlockSpec((None, bm, D), lambda i: (0, i, 0)),
                  pl.BlockSpec((None, bm, D), lambda i: (1, i, 0)),
                  row, row, row,
                  pl.BlockSpec((8, D), lambda i: (0, 0)),
                  pl.BlockSpec((D, D), lambda i: (0, 0)),
                  pl.BlockSpec((D, D), lambda i: (0, 0))],
        out_specs=[row, row],
        out_shape=[jax.ShapeDtypeStruct((Tp, D), F32), jax.ShapeDtypeStruct((Tp, D), BF16)],
        compiler_params=_cparams("parallel"),
    )(y2, y2, g, bonus, res, vecs, avg, w_o.astype(BF16))


def _router_kernel(x_ref, wh_ref, wl_ref, b_ref, gates_ref):
    x = x_ref[...]
    xh, xl = _split2(x)
    wh = wh_ref[...]
    logits = _dot(xh, wh) + _dot(xh, wl_ref[...]) + _dot(xl, wh) + b_ref[...]
    lane = lax.broadcasted_iota(jnp.int32, logits.shape, 1)
    mx = jnp.max(logits, axis=-1, keepdims=True)
    ex = jnp.exp(logits - mx)
    probs = ex / jnp.sum(ex, axis=-1, keepdims=True)
    big = jnp.int32(LANES)
    m1 = jnp.max(probs, axis=-1, keepdims=True)
    i1 = jnp.min(jnp.where(probs == m1, lane, big), axis=-1, keepdims=True)
    sel1 = lane == i1
    rest = jnp.where(sel1 | (lane >= N_EXPERTS), -1.0, probs)
    m2 = jnp.max(rest, axis=-1, keepdims=True)
    i2 = jnp.min(jnp.where(rest == m2, lane, big), axis=-1, keepdims=True)
    sel2 = lane == i2
    tot = m1 + m2
    gates_ref[...] = jnp.where(sel1, m1 / tot, 0.0) + jnp.where(sel2, m2 / tot, 0.0)


def moe_router(h, w_router, b_router, bm):
    Tp, D = h.shape
    E = w_router.shape[1]
    wp = jnp.pad(w_router, ((0, 0), (0, LANES - E)))
    wh = wp.astype(BF16)
    wl = (wp - wh.astype(F32)).astype(BF16)
    bp = jnp.concatenate([b_router, jnp.full((LANES - E,), NEG_BIG, F32)]).reshape(1, LANES)
    return pl.pallas_call(
        _router_kernel,
        grid=(Tp // bm,),
        in_specs=[pl.BlockSpec((bm, D), lambda i: (i, 0)),
                  pl.BlockSpec((D, LANES), lambda i: (0, 0)),
                  pl.BlockSpec((D, LANES), lambda i: (0, 0)),
                  pl.BlockSpec((1, LANES), lambda i: (0, 0))],
        out_specs=pl.BlockSpec((bm, LANES), lambda i: (i, 0)),
        out_shape=jax.ShapeDtypeStruct((Tp, LANES), F32),
        compiler_params=_cparams("parallel"),
    )(h, wh, wl, bp)


def _largest_divisor(n, cap, mult):
    best = mult
    for c in range(mult, cap + 1, mult):
        if n % c == 0:
            best = c
    return best


def kernel(x, meta, ln_g, ln_b, attn_w_in, attn_w_o, attn_lam_q1, attn_lam_k1, attn_lam_q2, attn_lam_k2, attn_subln_g, ffn_w_gate, ffn_w_up, ffn_w_down, rw_mu, rw_w_rkv, rw_w0, rw_w1, rw_w2, rw_a0, rw_a1, rw_a2, rw_g1, rw_g2, rw_k_k, rw_k_a, rw_r_k, rw_lnx_g, rw_lnx_b, rw_w_o, moe_w_router, moe_b_router, moe_w_gate, moe_w_up, moe_w_down):
    B, S, D = x.shape
    L = S + N_META
    Lp = -(-L // SEQ_MULTIPLE) * SEQ_MULTIPLE
    Tp = B * Lp
    BM = 640
    BM_RW = 256

    h = jnp.concatenate([jnp.broadcast_to(meta[None].astype(x.dtype), (B, N_META, D)), x,
                         jnp.zeros((B, Lp - L, D), x.dtype)], axis=1).reshape(Tp, D)

    qkv = matmul(h, attn_w_in[0].astype(BF16), BF16, BM, 1024)
    o = diff_attention_core(qkv, attn_lam_q1[0], attn_lam_k1[0], attn_lam_q2[0], attn_lam_k2[0],
                            attn_subln_g[0], B, Lp, L, 0, bq=640, bk=640)
    h, hb = matmul_res_ln(o, attn_w_o[0].astype(BF16), h, ln_g[0, 0], ln_b[0, 0], BM, D)
    F = ffn_w_gate.shape[-1]
    act = ffn_up(hb, ffn_w_gate.astype(BF16), ffn_w_up.astype(BF16), None, BM,
                 _largest_divisor(F, 1792, LANES))
    h, hb = matmul_res_ln(act, ffn_w_down[0].astype(BF16), h, ln_g[0, 1], ln_b[0, 1], BM,
                          _largest_divisor(F, 2048, LANES))

    r, v, kk, g, bonus, lw, kd, b = rwkv_prep(
        h, rw_mu[0], rw_w_rkv[0], rw_w1[0], rw_w2[0], rw_a1[0], rw_a2[0], rw_g1[0], rw_g2[0],
        rw_w0[0], rw_a0[0], rw_k_k[0], rw_k_a[0], rw_r_k[0], Lp, L, BM_RW)
    y2 = rwkv_scan(r, v, kk, lw, kd, b, B, Lp)
    h, hb = rwkv_post(y2, g, bonus, h, rw_lnx_g[0], rw_lnx_b[0], rw_w_o[0], ln_g[1, 0], ln_b[1, 0],
                      BM_RW)

    gates = moe_router(h, moe_w_router[0], moe_b_router[0], BM)
    Fe = moe_w_gate.shape[-1]
    E = moe_w_gate.shape[1]
    act = ffn_up(hb, moe_w_gate[0].astype(BF16), moe_w_up[0].astype(BF16), gates, BM,
                 _largest_divisor(Fe, 1792, LANES))
    h, hb = matmul_res_ln(act, moe_w_down[0].reshape(E * Fe, D).astype(BF16), h,
                          ln_g[1, 1], ln_b[1, 1], BM, _largest_divisor(Fe, 2048, LANES))

    return h.reshape(B, Lp, D)[:, N_META:L]
```

```python
import functools
import math

import jax
import jax.numpy as jnp
from jax import lax
from jax.experimental import pallas as pl
from jax.experimental.pallas import tpu as pltpu

F32 = jnp.float32
BF16 = jnp.bfloat16

N_META = 16
DEPTH = 2
LN_EPS = 1e-5
DEEPNORM_ALPHA = (2.0 * DEPTH) ** 0.25

DA_HEADS = 8
DA_HEAD_DIM = 64
DA_V_DIM = 128
SUBLN_EPS = 1e-5

RW_HEAD_DIM = 64
RW_LNX_EPS = 64e-5
RW_CHUNK = 64
RW_GROUP = 256

N_EXPERTS = 8
LANES = 128

SEQ_MULTIPLE = 1280
VMEM_LIMIT = 56 * 1024 * 1024
NEG_BIG = -1e30
LOG2E = 1.4426950408889634
ROWS = 32


def _cparams(*sem):
    return pltpu.CompilerParams(dimension_semantics=sem, vmem_limit_bytes=VMEM_LIMIT)


def _dot(a, b):
    return jnp.dot(a, b, preferred_element_type=F32)


def _dot_nt(a, b):
    return lax.dot_general(a, b, (((1,), (1,)), ((), ())), preferred_element_type=F32)


def _dot_tn(a, b):
    return lax.dot_general(a, b, (((0,), (0,)), ((), ())), preferred_element_type=F32)


def _split2(x):
    hi = x.astype(BF16)
    lo = (x - hi.astype(F32)).astype(BF16)
    return hi, lo


def _layer_norm_rows(z, g, b):
    mu = jnp.mean(z, -1, keepdims=True)
    zc = z - mu
    var = jnp.mean(zc * zc, -1, keepdims=True)
    return zc * lax.rsqrt(var + LN_EPS) * g + b


def _mm_kernel(a_ref, w_ref, o_ref):
    o_ref[...] = _dot(a_ref[...].astype(BF16), w_ref[...]).astype(o_ref.dtype)


def matmul(a, w, out_dtype, bm, bn):
    M, K = a.shape
    N = w.shape[1]
    return pl.pallas_call(
        _mm_kernel,
        grid=(N // bn, M // bm),
        in_specs=[pl.BlockSpec((bm, K), lambda j, i: (i, 0)),
                  pl.BlockSpec((K, bn), lambda j, i: (0, j))],
        out_specs=pl.BlockSpec((bm, bn), lambda j, i: (i, j)),
        out_shape=jax.ShapeDtypeStruct((M, N), out_dtype),
        compiler_params=_cparams("parallel", "parallel"),
    )(a, w)


def _mm_res_ln_kernel(a_ref, w_ref, res_ref, g_ref, b_ref, o_ref, ob_ref, acc_ref):
    k = pl.program_id(1)

    @pl.when(k == 0)
    def _():
        acc_ref[...] = jnp.zeros_like(acc_ref)

    acc_ref[...] += _dot(a_ref[...].astype(BF16), w_ref[...])

    @pl.when(k == pl.num_programs(1) - 1)
    def _():
        z = DEEPNORM_ALPHA * res_ref[...] + acc_ref[...]
        o = _layer_norm_rows(z, g_ref[...], b_ref[...])
        o_ref[...] = o
        ob_ref[...] = o.astype(BF16)


def matmul_res_ln(a, w, res, g, b, bm, bk):
    M, K = a.shape
    N = w.shape[1]
    return pl.pallas_call(
        _mm_res_ln_kernel,
        grid=(M // bm, K // bk),
        in_specs=[pl.BlockSpec((bm, bk), lambda i, k: (i, k)),
                  pl.BlockSpec((bk, N), lambda i, k: (k, 0)),
                  pl.BlockSpec((bm, N), lambda i, k: (i, 0)),
                  pl.BlockSpec((1, N), lambda i, k: (0, 0)),
                  pl.BlockSpec((1, N), lambda i, k: (0, 0))],
        out_specs=[pl.BlockSpec((bm, N), lambda i, k: (i, 0)),
                   pl.BlockSpec((bm, N), lambda i, k: (i, 0))],
        out_shape=[jax.ShapeDtypeStruct((M, N), F32), jax.ShapeDtypeStruct((M, N), BF16)],
        scratch_shapes=[pltpu.VMEM((bm, N), F32)],
        compiler_params=_cparams("parallel", "arbitrary"),
    )(a, w, res, g.reshape(1, N), b.reshape(1, N))


def _ffn_up_kernel(x_ref, wg_ref, wu_ref, gates_ref, o_ref, *, nj, use_gates):
    x = x_ref[...]
    g = _dot(x, wg_ref[...])
    u = _dot(x, wu_ref[...])
    act = g * jax.nn.sigmoid(g) * u
    if use_gates:
        e = pl.program_id(0) // nj
        gates = gates_ref[...]
        lane = lax.broadcasted_iota(jnp.int32, gates.shape, 1)
        gcol = jnp.sum(jnp.where(lane == e, gates, 0.0), axis=-1, keepdims=True)
        act = act * gcol
    o_ref[...] = act.astype(o_ref.dtype)


def ffn_up(xb, wg, wu, gates, bm, bn):
    M, D = xb.shape
    E, _, Fe = wg.shape
    nj = Fe // bn
    use_gates = gates is not None
    if gates is None:
        gates = jnp.zeros((M, LANES), F32)
    return pl.pallas_call(
        functools.partial(_ffn_up_kernel, nj=nj, use_gates=use_gates),
        grid=(E * nj, M // bm),
        in_specs=[pl.BlockSpec((bm, D), lambda n, i: (i, 0)),
                  pl.BlockSpec((None, D, bn), lambda n, i: (n // nj, 0, n % nj)),
                  pl.BlockSpec((None, D, bn), lambda n, i: (n // nj, 0, n % nj)),
                  pl.BlockSpec((bm, LANES), lambda n, i: (i, 0))],
        out_specs=pl.BlockSpec((bm, bn), lambda n, i: (i, n)),
        out_shape=jax.ShapeDtypeStruct((M, E * Fe), BF16),
        compiler_params=_cparams("parallel", "parallel"),
    )(xb, wg, wu, gates)


def _attn_kernel(slopes_ref, q_ref, k_ref, v_ref, lq1_ref, lk1_ref, lq2_ref, lk2_ref, sg_ref,
                 o_ref, va_ref, qm_ref, s0_ref, s1_ref, p0_ref, p1_ref, m_ref, al0_ref, al1_ref,
                 sh0_ref, sh1_ref, cb0_ref, cb1_ref, acc_ref,
                 *, bq, bk, nk, seq_len, lam_init):
    hd = DA_HEAD_DIM
    s_ref, p_ref, al_ref = (s0_ref, s1_ref), (p0_ref, p1_ref), (al0_ref, al1_ref)
    sh_ref, cb_ref = (sh0_ref, sh1_ref), (cb0_ref, cb1_ref)
    h = pl.program_id(1)
    qi = pl.program_id(2)
    cl = slopes_ref[h] * LOG2E
    ncc = bk // LANES

    @pl.when(qi == 0)
    def _():
        def build(jb, carry):
            off = pl.multiple_of(jb * bk, bk)
            lane = lax.broadcasted_iota(jnp.int32, (bk, DA_V_DIM), 1)
            va_ref[pl.ds(off, bk), 0:DA_V_DIM] = v_ref[pl.ds(off, bk), :]
            va_ref[pl.ds(off, bk), DA_V_DIM:] = jnp.where(lane == 0, 1.0, 0.0).astype(BF16)
            return carry
        lax.fori_loop(0, nk, build, 0)

    q = q_ref[...].astype(F32) * (hd ** -0.5 * LOG2E)
    lane = lax.broadcasted_iota(jnp.int32, (bq, 2 * hd), 1)
    qm_ref[0] = jnp.where(lane < hd, q, 0.0).astype(BF16)
    qm_ref[1] = jnp.where(lane >= hd, q, 0.0).astype(BF16)

    m_ref[...] = jnp.full(m_ref.shape, NEG_BIG, F32)
    acc_ref[...] = jnp.zeros(acc_ref.shape, F32)
    q0 = qi * bq
    n_left = q0 // bk
    first_right = (q0 + bq + bk - 1) // bk
    n_over = first_right - n_left
    n_other = nk - n_over

    def bias_pass(slot, off):
        for r in range(bq // ROWS):
            rows = slice(r * ROWS, (r + 1) * ROWS)
            qpos = q0 + r * ROWS + lax.broadcasted_iota(jnp.int32, (ROWS, LANES), 0)
            kpos = off + lax.broadcasted_iota(jnp.int32, (ROWS, LANES), 1)
            for cc in range(ncc):
                cols = slice(cc * LANES, (cc + 1) * LANES)
                bias = cl * jnp.abs(qpos - (kpos + cc * LANES)).astype(F32)
                for c in range(2):
                    s_ref[slot][c, rows, cols] = s_ref[slot][c, rows, cols] - bias

    def col_bias(slot, off, side):
        jr = lax.broadcasted_iota(jnp.int32, (8, bk), 1)
        cb = jnp.where(off + jr < seq_len, 0.0, -NEG_BIG)
        if side is not None:
            cb = cb + cl * jnp.where(side == 1, jr, bk - 1 - jr).astype(F32)
        cb_ref[slot][...] = cb

    def max_pass(slot, row_slope, row_const):
        for r in range(bq // ROWS):
            rows = slice(r * ROWS, (r + 1) * ROWS)
            ir = (r * ROWS + lax.broadcasted_iota(jnp.int32, (ROWS, LANES), 0)).astype(F32)
            roff = row_slope * ir - row_const
            for c in range(2):
                mx = s_ref[slot][c, rows, 0:LANES] - cb_ref[slot][0:1, 0:LANES]
                for cc in range(1, ncc):
                    cols = slice(cc * LANES, (cc + 1) * LANES)
                    mx = jnp.maximum(mx, s_ref[slot][c, rows, cols] - cb_ref[slot][0:1, cols])
                mx = jnp.max(mx, axis=-1, keepdims=True)
                m_old = m_ref[c, rows, :]
                m_new = jnp.maximum(m_old, mx + roff)
                m_ref[c, rows, :] = m_new
                al_ref[slot][c, rows, :] = jnp.exp2(m_old - m_new)
                sh_ref[slot][c, rows, :] = m_new - roff

    def exp_pass(slot):
        for r in range(bq // ROWS):
            rows = slice(r * ROWS, (r + 1) * ROWS)
            for c in range(2):
                shift = sh_ref[slot][c, rows, :]
                for cc in range(ncc):
                    cols = slice(cc * LANES, (cc + 1) * LANES)
                    x = s_ref[slot][c, rows, cols] - cb_ref[slot][0:1, cols] - shift
                    p_ref[slot][c, rows, cols] = jnp.exp2(x.astype(BF16))

    def rescale(slot):
        for c in range(2):
            al = al_ref[slot][c]
            acc_ref[c] = jnp.concatenate([al, al], axis=1) * acc_ref[c]

    def pv_block(slot, off):
        vb = va_ref[pl.ds(off, bk), :]
        for c in range(2):
            acc_ref[c] += _dot(p_ref[slot][c], vb)

    def over_body(j, carry):
        off = pl.multiple_of(j * bk, bk)
        for c in range(2):
            s_ref[0][c] = _dot_nt(qm_ref[c], k_ref[pl.ds(off, bk), :])
        bias_pass(0, off)
        col_bias(0, off, None)
        max_pass(0, 0.0, 0.0)
        exp_pass(0)
        rescale(0)
        pv_block(0, off)
        return carry

    lax.fori_loop(n_left, first_right, over_body, 0)

    def block_of(t):
        t = jnp.clip(t, 0, jnp.maximum(n_other - 1, 0))
        j = jnp.minimum(jnp.where(t < n_left, t, t + n_over), nk - 1)
        off = pl.multiple_of(j * bk, bk)
        side = jnp.where(j < n_left, 0, 1)
        return off, side

    def qk_max_block(slot, t):
        off, side = block_of(t)
        for c in range(2):
            s_ref[slot][c] = _dot_nt(qm_ref[c], k_ref[pl.ds(off, bk), :])
        col_bias(slot, off, side)
        gap = jnp.where(side == 1, off - q0, q0 - off - (bk - 1)).astype(F32)
        max_pass(slot, jnp.where(side == 1, cl, -cl), cl * gap)

    @pl.when(n_other > 0)
    def _():
        qk_max_block(0, 0)

    def step(t, slot):
        qk_max_block(1 - slot, t + 1)
        exp_pass(slot)
        rescale(slot)
        pv_block(slot, block_of(t)[0])

    def pair(u, carry):
        step(2 * u, 0)
        step(2 * u + 1, 1)
        return carry

    lax.fori_loop(0, n_other // 2, pair, 0)

    @pl.when(n_other % 2 == 1)
    def _():
        step(n_other - 1, 0)

    lam = (jnp.exp(jnp.sum(lq1_ref[...] * lk1_ref[...], axis=-1, keepdims=True))
           - jnp.exp(jnp.sum(lq2_ref[...] * lk2_ref[...], axis=-1, keepdims=True)) + lam_init)
    l1 = acc_ref[0, :, DA_V_DIM:DA_V_DIM + 1]
    l2 = acc_ref[1, :, DA_V_DIM:DA_V_DIM + 1]
    o = acc_ref[0, :, 0:DA_V_DIM] / l1 - lam * (acc_ref[1, :, 0:DA_V_DIM] / l2)
    o = o * lax.rsqrt(jnp.mean(o * o, -1, keepdims=True) + SUBLN_EPS) * sg_ref[...] * (1.0 - lam_init)
    o_ref[...] = o.astype(o_ref.dtype)


def diff_attention_core(qkv, lq1, lk1, lq2, lk2, subln_g, B, Lp, seq_len, layer_idx, bq, bk):
    Tp, D3 = qkv.shape
    D = D3 // 3
    H = DA_HEADS
    nq = Lp // bq
    nk = Lp // bk
    lam_init = 0.8 - 0.6 * math.exp(-0.3 * layer_idx)
    slopes = jnp.asarray([2.0 ** (-(8.0 / H) * (i + 1)) for i in range(H)], F32)
    hd = DA_HEAD_DIM
    grid_spec = pltpu.PrefetchScalarGridSpec(
        num_scalar_prefetch=1,
        grid=(B, H, nq),
        in_specs=[pl.BlockSpec((bq, 2 * hd), lambda b, h, i, s: (b * nq + i, h)),
                  pl.BlockSpec((Lp, 2 * hd), lambda b, h, i, s: (b, H + h),
                               pipeline_mode=pl.Buffered(1)),
                  pl.BlockSpec((Lp, DA_V_DIM), lambda b, h, i, s: (b, 2 * H + h),
                               pipeline_mode=pl.Buffered(1)),
                  pl.BlockSpec((1, hd), lambda b, h, i, s: (0, 0)),
                  pl.BlockSpec((1, hd), lambda b, h, i, s: (0, 0)),
                  pl.BlockSpec((1, hd), lambda b, h, i, s: (0, 0)),
                  pl.BlockSpec((1, hd), lambda b, h, i, s: (0, 0)),
                  pl.BlockSpec((1, DA_V_DIM), lambda b, h, i, s: (0, 0))],
        out_specs=pl.BlockSpec((bq, DA_V_DIM), lambda b, h, i, s: (b * nq + i, h)),
        scratch_shapes=[pltpu.VMEM((Lp, 2 * DA_V_DIM), BF16),
                        pltpu.VMEM((2, bq, 2 * hd), BF16),
                        pltpu.VMEM((2, bq, bk), F32),
                        pltpu.VMEM((2, bq, bk), F32),
                        pltpu.VMEM((2, bq, bk), BF16),
                        pltpu.VMEM((2, bq, bk), BF16),
                        pltpu.VMEM((2, bq, LANES), F32),
                        pltpu.VMEM((2, bq, LANES), F32),
                        pltpu.VMEM((2, bq, LANES), F32),
                        pltpu.VMEM((2, bq, LANES), F32),
                        pltpu.VMEM((2, bq, LANES), F32),
                        pltpu.VMEM((8, bk), F32),
                        pltpu.VMEM((8, bk), F32),
                        pltpu.VMEM((2, bq, 2 * DA_V_DIM), F32)],
    )
    return pl.pallas_call(
        functools.partial(_attn_kernel, bq=bq, bk=bk, nk=nk, seq_len=seq_len, lam_init=lam_init),
        grid_spec=grid_spec,
        out_shape=jax.ShapeDtypeStruct((Tp, D), BF16),
        compiler_params=_cparams("parallel", "parallel", "arbitrary"),
    )(slopes, qkv, qkv, qkv, lq1.reshape(1, hd), lk1.reshape(1, hd), lq2.reshape(1, hd),
      lk2.reshape(1, hd), subln_g.reshape(1, DA_V_DIM))


def _seg_sum(x, ones_bd):
    hi, lo = _split2(x)
    return _dot(hi, ones_bd) + _dot(lo, ones_bd)


def _rw_prep_kernel(x_ref, xp_ref, xn_ref, mu_ref, wrkv_ref, w1_ref, w2_ref, a1_ref, a2_ref,
                    g1_ref, g2_ref, vec_ref, ones_ref,
                    r_ref, v_ref, kk_ref, g_ref, bonus_ref, lw_ref, kd_ref, b_ref,
                    *, bm, Lp, seq_len, lora_w, lora_a):
    i = pl.program_id(0)
    x = x_ref[...]
    row = lax.broadcasted_iota(jnp.int32, (bm, 1), 0)
    pos = (i * bm) % Lp + row
    prev_last = xp_ref[7:8, :]
    next_first = xn_ref[0:1, :]
    x_prev = jnp.where(row == 0, prev_last, pltpu.roll(x, 1, 0))
    x_next = jnp.where(row == bm - 1, next_first, pltpu.roll(x, bm - 1, 0))
    x_prev = jnp.where(pos == 0, 0.0, x_prev)
    x_next = jnp.where(pos == seq_len - 1, 0.0, x_next)
    dxp = x_prev - x
    dxn = x_next - x
    valid = pos < seq_len

    def mix(n):
        return (x + mu_ref[n:n + 1, :] * dxp + mu_ref[6 + n:7 + n, :] * dxn).astype(BF16)

    r = _dot(mix(0), wrkv_ref[0])
    k = _dot(mix(1), wrkv_ref[1])
    v = _dot(mix(2), wrkv_ref[2])

    lane = lax.broadcasted_iota(jnp.int32, (bm, 2 * lora_w), 1)
    th = jnp.tanh(_dot(mix(3), w1_ref[...]))
    lw = [_dot(jnp.where((lane >= n * lora_w) & (lane < (n + 1) * lora_w), th, 0.0).astype(BF16),
               w2_ref[...]) for n in range(2)]
    lane = lax.broadcasted_iota(jnp.int32, (bm, 2 * lora_a), 1)
    ah = _dot(mix(4), a1_ref[...])
    la = [_dot(jnp.where((lane >= n * lora_a) & (lane < (n + 1) * lora_a), ah, 0.0).astype(BF16),
               a2_ref[...]) for n in range(2)]
    g = _dot(jax.nn.sigmoid(_dot(mix(5), g1_ref[...])).astype(BF16), g2_ref[...])

    k_k = vec_ref[4:5, :]
    k_a = vec_ref[5:6, :]
    r_k = vec_ref[6:7, :]
    ones_bd = ones_ref[...]

    kkr = k * k_k
    nrm = jnp.sqrt(_seg_sum(kkr * kkr, ones_bd))
    kk = kkr / jnp.maximum(nrm, 1e-12)
    kk = jnp.where(valid, kk, 0.0)
    vz = jnp.where(valid, v, 0.0)

    kd_sum = jnp.zeros_like(k)
    for n in range(2):
        z = -(vec_ref[n:n + 1, :] + lw[n])
        softplus = jnp.maximum(z, 0.0) + jnp.log(1.0 + jnp.exp(-jnp.abs(z)))
        w_log = -softplus - 0.5
        lw_ref[n] = jnp.where(valid, -jnp.exp(w_log), 0.0)
        a = jax.nn.sigmoid(vec_ref[2 + n:3 + n, :] + la[n])
        kd = k * (1.0 + (a - 1.0) * k_a)
        kd_sum = kd_sum + kd
        kd_ref[n] = jnp.where(valid, kd, 0.0)
        b_ref[n] = kk * a

    bonus = _seg_sum(r * kd_sum * r_k, ones_bd) * v
    r_ref[...] = r
    v_ref[...] = vz
    kk_ref[...] = kk
    g_ref[...] = g
    bonus_ref[...] = bonus


def rwkv_prep(h, mu, w_rkv, w1, w2, a1, a2, g1, g2, w0, a0, k_k, k_a, r_k, Lp, seq_len, bm):
    Tp, D = h.shape
    lora_w = w1.shape[-1]
    lora_a = a1.shape[-1]
    lora_g = g1.shape[-1]
    gpad = -(-lora_g // LANES) * LANES
    mu12 = mu.reshape(12, D)
    w1c = jnp.concatenate([w1[0], w1[1]], axis=1).astype(BF16)
    w2c = jnp.concatenate([w2[0], w2[1]], axis=0).astype(BF16)
    a1c = jnp.concatenate([a1[0], a1[1]], axis=1).astype(BF16)
    a2c = jnp.concatenate([a2[0], a2[1]], axis=0).astype(BF16)
    g1p = jnp.pad(g1, ((0, 0), (0, gpad - lora_g))).astype(BF16)
    g2p = jnp.pad(g2, ((0, gpad - lora_g), (0, 0))).astype(BF16)
    vecs = jnp.concatenate([w0, a0, k_k.reshape(1, D), k_a.reshape(1, D), r_k.reshape(1, D),
                            jnp.zeros((1, D), F32)], axis=0)
    seg = jnp.arange(D) // RW_HEAD_DIM
    ones_bd = (seg[:, None] == seg[None, :]).astype(BF16)
    nb8 = bm // 8
    last8 = Tp // 8 - 1
    const2 = lambda i: (0, 0)
    const3 = lambda i: (0, 0, 0)
    row = pl.BlockSpec((bm, D), lambda i: (i, 0))
    row2 = pl.BlockSpec((2, bm, D), lambda i: (0, i, 0))
    sds = jax.ShapeDtypeStruct((Tp, D), F32)
    sds2 = jax.ShapeDtypeStruct((2, Tp, D), F32)
    return pl.pallas_call(
        functools.partial(_rw_prep_kernel, bm=bm, Lp=Lp, seq_len=seq_len, lora_w=lora_w, lora_a=lora_a),
        grid=(Tp // bm,),
        in_specs=[row,
                  pl.BlockSpec((8, D), lambda i: (jnp.maximum(i * nb8 - 1, 0), 0)),
                  pl.BlockSpec((8, D), lambda i: (jnp.minimum((i + 1) * nb8, last8), 0)),
                  pl.BlockSpec((12, D), const2),
                  pl.BlockSpec((3, D, D), const3),
                  pl.BlockSpec((D, 2 * lora_w), const2),
                  pl.BlockSpec((2 * lora_w, D), const2),
                  pl.BlockSpec((D, 2 * lora_a), const2),
                  pl.BlockSpec((2 * lora_a, D), const2),
                  pl.BlockSpec((D, gpad), const2),
                  pl.BlockSpec((gpad, D), const2),
                  pl.BlockSpec((8, D), const2),
                  pl.BlockSpec((D, D), const2)],
        out_specs=[row, row, row, row, row, row2, row2, row2],
        out_shape=[sds, sds, sds, sds, sds, sds2, sds2, sds2],
        compiler_params=_cparams("parallel"),
    )(h, h, h, mu12, w_rkv.astype(BF16), w1c, w2c, a1c, a2c, g1p, g2p, vecs, ones_bd)


def _rw_scan_kernel(r_ref, v_ref, kk_ref, lw_ref, kd_ref, b_ref, y_ref, s_ref, *, C, D):
    d = pl.program_id(1)
    c = pl.program_id(2)
    G_ = RW_GROUP
    hpg = G_ // C
    fwd = d == 0

    @pl.when(c == 0)
    def _():
        s_ref[...] = jnp.zeros_like(s_ref)

    sgn = jnp.where(fwd, 1, -1)
    ri = lax.broadcasted_iota(jnp.int32, (C, C), 0)
    ci = lax.broadcasted_iota(jnp.int32, (C, C), 1)
    tri = jnp.where((ri - ci) * sgn >= 0, 1.0, 0.0).astype(BF16)

    lw = lw_ref[...]
    hi = lw.astype(BF16)
    r1 = lw - hi.astype(F32)
    mid = r1.astype(BF16)
    lo = (r1 - mid.astype(F32)).astype(BF16)
    G = _dot(tri, hi) + _dot(tri, mid) + _dot(tri, lo)
    g_end = jnp.where(fwd, G[C - 1:C, :], G[0:1, :])
    e_pos = jnp.exp(G)
    e_neg = jnp.exp(-G)
    e_exc = jnp.exp(G - lw)
    e_rem = jnp.exp(g_end - G)
    kk = kk_ref[...]
    b = b_ref[...]
    kd = kd_ref[...]
    At = (-kk * e_exc).astype(BF16)
    Bt = (b * e_neg).astype(BF16)
    Kt = (kd * e_neg).astype(BF16)
    Rt = (r_ref[...] * e_pos).astype(BF16)
    Bh = (b * e_rem).astype(BF16)
    Kh = (kd * e_rem).astype(BF16)
    Vb = v_ref[...].astype(BF16)
    gam = jnp.exp(g_end)

    R = lax.broadcasted_iota(jnp.int32, (G_, G_), 0)
    Cc = lax.broadcasted_iota(jnp.int32, (G_, G_), 1)
    blk = (R // C) == (Cc // RW_HEAD_DIM)
    dt = jnp.where((R // C) == (Cc // C), (R % C - Cc % C) * sgn, -1)
    strict = dt > 0
    incl = dt >= 0
    eye = jnp.where(R == Cc, 1.0, 0.0)
    zero_b = jnp.zeros((G_, G_), BF16)

    def expand(x, q):
        xq = x[:, q * G_:(q + 1) * G_]
        return jnp.where(blk, jnp.concatenate([xq] * hpg, axis=0), zero_b)

    groups = range(D // G_)
    Ae = [expand(At, q) for q in groups]
    Be = [expand(Bt, q) for q in groups]
    Ke = [expand(Kt, q) for q in groups]
    Re = [expand(Rt, q) for q in groups]
    Ve = [expand(Vb, q) for q in groups]
    Sb = [s_ref[q].astype(BF16) for q in groups]
    N = [jnp.where(strict, _dot_nt(Ae[q], Be[q]), 0.0) for q in groups]
    P = [eye + N[q] for q in groups]
    Xb = [N[q].astype(BF16) for q in groups]
    Aak = [jnp.where(strict, _dot_nt(Ae[q], Ke[q]), 0.0).astype(BF16) for q in groups]
    rhs = [(_dot_nt(Ae[q], Sb[q]) + _dot(Aak[q], Ve[q])).astype(BF16) for q in groups]
    p = 1
    while 2 * p < C:
        Xb = [_dot(Xb[q], Xb[q]).astype(BF16) for q in groups]
        P = [P[q] + _dot(P[q].astype(BF16), Xb[q]) for q in groups]
        p *= 2
    Ub = [_dot(P[q].astype(BF16), rhs[q]).astype(BF16) for q in groups]
    Arb = [jnp.where(incl, _dot_nt(Re[q], Be[q]), 0.0).astype(BF16) for q in groups]
    Ark = [jnp.where(incl, _dot_nt(Re[q], Ke[q]), 0.0).astype(BF16) for q in groups]
    for q in groups:
        Y = _dot_nt(Re[q], Sb[q]) + _dot(Arb[q], Ub[q]) + _dot(Ark[q], Ve[q])
        yq = Y[0:C, :]
        for hb in range(1, hpg):
            yq = yq + Y[hb * C:(hb + 1) * C, :]
        y_ref[:, q * G_:(q + 1) * G_] = yq
    for q in groups:
        s_ref[q] = (s_ref[q] * gam[:, q * G_:(q + 1) * G_] + _dot_tn(Ub[q], expand(Bh, q))
                    + _dot_tn(Ve[q], expand(Kh, q)))


def rwkv_scan(r, v, kk, lw, kd, b, B, Lp):
    Tp, D = r.shape
    C = RW_CHUNK
    nch = Lp // C

    def rb(bi, d, c):
        return bi * nch + jnp.where(d == 0, c, nch - 1 - c)

    shared = pl.BlockSpec((C, D), lambda bi, d, c: (rb(bi, d, c), 0))
    perdir = pl.BlockSpec((None, C, D), lambda bi, d, c: (d, rb(bi, d, c), 0))
    return pl.pallas_call(
        functools.partial(_rw_scan_kernel, C=C, D=D),
        grid=(B, 2, nch),
        in_specs=[shared, shared, shared, perdir, perdir, perdir],
        out_specs=perdir,
        out_shape=jax.ShapeDtypeStruct((2, Tp, D), F32),
        scratch_shapes=[pltpu.VMEM((D // RW_GROUP, RW_GROUP, RW_GROUP), F32)],
        compiler_params=_cparams("parallel", "parallel", "arbitrary"),
    )(r, v, kk, lw, kd, b)


def _rw_post_kernel(yf_ref, yb_ref, g_ref, bonus_ref, res_ref, vec_ref, avg_ref, wo_ref,
                    o_ref, ob_ref):
    avg = avg_ref[...]
    y = yf_ref[...] + yb_ref[...]
    ym = _seg_sum(y, avg)
    yc = y - ym
    yv = _seg_sum(yc * yc, avg)
    yn = yc * lax.rsqrt(yv + RW_LNX_EPS) * vec_ref[0:1, :] + vec_ref[1:2, :]
    out = ((yn + bonus_ref[...]) * g_ref[...]).astype(BF16)
    z = DEEPNORM_ALPHA * res_ref[...] + _dot(out, wo_ref[...])
    o = _layer_norm_rows(z, vec_ref[2:3, :], vec_ref[3:4, :])
    o_ref[...] = o
    ob_ref[...] = o.astype(BF16)


def rwkv_post(y2, g, bonus, res, lnx_g, lnx_b, w_o, ln_g, ln_b, bm):
    Tp, D = res.shape
    vecs = jnp.concatenate([lnx_g.reshape(1, D), lnx_b.reshape(1, D), ln_g.reshape(1, D),
                            ln_b.reshape(1, D), jnp.zeros((4, D), F32)], axis=0)
    seg = jnp.arange(D) // RW_HEAD_DIM
    avg = ((seg[:, None] == seg[None, :]).astype(F32) / RW_HEAD_DIM).astype(BF16)
    row = pl.BlockSpec((bm, D), lambda i: (i, 0))
    return pl.pallas_call(
        _rw_post_kernel,
        grid=(Tp // bm,),
        in_specs=[pl.BlockSpec((None, bm, D), lambda i: (0, i, 0)),
                  pl.BlockSpec((None, bm, D), lambda i: (1, i, 0)),
                  row, row, row,
                  pl.BlockSpec((8, D), lambda i: (0, 0)),
                  pl.BlockSpec((D, D), lambda i: (0, 0)),
                  pl.BlockSpec((D, D), lambda i: (0, 0))],
        out_specs=[row, row],
        out_shape=[jax.ShapeDtypeStruct((Tp, D), F32), jax.ShapeDtypeStruct((Tp, D), BF16)],
        compiler_params=_cparams("parallel"),
    )(y2, y2, g, bonus, res, vecs, avg, w_o.astype(BF16))


def _router_kernel(x_ref, wh_ref, wl_ref, b_ref, gates_ref):
    x = x_ref[...]
    xh, xl = _split2(x)
    wh = wh_ref[...]
    logits = _dot(xh, wh) + _dot(xh, wl_ref[...]) + _dot(xl, wh) + b_ref[...]
    lane = lax.broadcasted_iota(jnp.int32, logits.shape, 1)
    mx = jnp.max(logits, axis=-1, keepdims=True)
    ex = jnp.exp(logits - mx)
    probs = ex / jnp.sum(ex, axis=-1, keepdims=True)
    big = jnp.int32(LANES)
    m1 = jnp.max(probs, axis=-1, keepdims=True)
    i1 = jnp.min(jnp.where(probs == m1, lane, big), axis=-1, keepdims=True)
    sel1 = lane == i1
    rest = jnp.where(sel1 | (lane >= N_EXPERTS), -1.0, probs)
    m2 = jnp.max(rest, axis=-1, keepdims=True)
    i2 = jnp.min(jnp.where(rest == m2, lane, big), axis=-1, keepdims=True)
    sel2 = lane == i2
    tot = m1 + m2
    gates_ref[...] = jnp.where(sel1, m1 / tot, 0.0) + jnp.where(sel2, m2 / tot, 0.0)


def moe_router(h, w_router, b_router, bm):
    Tp, D = h.shape
    E = w_router.shape[1]
    wp = jnp.pad(w_router, ((0, 0), (0, LANES - E)))
    wh = wp.astype(BF16)
    wl = (wp - wh.astype(F32)).astype(BF16)
    bp = jnp.concatenate([b_router, jnp.full((LANES - E,), NEG_BIG, F32)]).reshape(1, LANES)
    return pl.pallas_call(
        _router_kernel,
        grid=(Tp // bm,),
        in_specs=[pl.BlockSpec((bm, D), lambda i: (i, 0)),
                  pl.BlockSpec((D, LANES), lambda i: (0, 0)),
                  pl.BlockSpec((D, LANES), lambda i: (0, 0)),
                  pl.BlockSpec((1, LANES), lambda i: (0, 0))],
        out_specs=pl.BlockSpec((bm, LANES), lambda i: (i, 0)),
        out_shape=jax.ShapeDtypeStruct((Tp, LANES), F32),
        compiler_params=_cparams("parallel"),
    )(h, wh, wl, bp)


def _largest_divisor(n, cap, mult):
    best = mult
    for c in range(mult, cap + 1, mult):
        if n % c == 0:
            best = c
    return best


def kernel(x, meta, ln_g, ln_b, attn_w_in, attn_w_o, attn_lam_q1, attn_lam_k1, attn_lam_q2, attn_lam_k2, attn_subln_g, ffn_w_gate, ffn_w_up, ffn_w_down, rw_mu, rw_w_rkv, rw_w0, rw_w1, rw_w2, rw_a0, rw_a1, rw_a2, rw_g1, rw_g2, rw_k_k, rw_k_a, rw_r_k, rw_lnx_g, rw_lnx_b, rw_w_o, moe_w_router, moe_b_router, moe_w_gate, moe_w_up, moe_w_down):
    B, S, D = x.shape
    L = S + N_META
    Lp = -(-L // SEQ_MULTIPLE) * SEQ_MULTIPLE
    Tp = B * Lp
    BM = 640
    BM_RW = 256

    h = jnp.concatenate([jnp.broadcast_to(meta[None].astype(x.dtype), (B, N_META, D)), x,
                         jnp.zeros((B, Lp - L, D), x.dtype)], axis=1).reshape(Tp, D)

    qkv = matmul(h, attn_w_in[0].astype(BF16), BF16, BM, 1024)
    o = diff_attention_core(qkv, attn_lam_q1[0], attn_lam_k1[0], attn_lam_q2[0], attn_lam_k2[0],
                            attn_subln_g[0], B, Lp, L, 0, bq=640, bk=1280)
    h, hb = matmul_res_ln(o, attn_w_o[0].astype(BF16), h, ln_g[0, 0], ln_b[0, 0], BM, D)
    F = ffn_w_gate.shape[-1]
    act = ffn_up(hb, ffn_w_gate.astype(BF16), ffn_w_up.astype(BF16), None, BM,
                 _largest_divisor(F, 1792, LANES))
    h, hb = matmul_res_ln(act, ffn_w_down[0].astype(BF16), h, ln_g[0, 1], ln_b[0, 1], BM,
                          _largest_divisor(F, 2048, LANES))

    r, v, kk, g, bonus, lw, kd, b = rwkv_prep(
        h, rw_mu[0], rw_w_rkv[0], rw_w1[0], rw_w2[0], rw_a1[0], rw_a2[0], rw_g1[0], rw_g2[0],
        rw_w0[0], rw_a0[0], rw_k_k[0], rw_k_a[0], rw_r_k[0], Lp, L, BM_RW)
    y2 = rwkv_scan(r, v, kk, lw, kd, b, B, Lp)
    h, hb = rwkv_post(y2, g, bonus, h, rw_lnx_g[0], rw_lnx_b[0], rw_w_o[0], ln_g[1, 0], ln_b[1, 0],
                      BM_RW)

    gates = moe_router(h, moe_w_router[0], moe_b_router[0], BM)
    Fe = moe_w_gate.shape[-1]
    E = moe_w_gate.shape[1]
    act = ffn_up(hb, moe_w_gate[0].astype(BF16), moe_w_up[0].astype(BF16), gates, BM,
                 _largest_divisor(Fe, 1792, LANES))
    h, hb = matmul_res_ln(act, moe_w_down[0].reshape(E * Fe, D).astype(BF16), h,
                          ln_g[1, 1], ln_b[1, 1], BM, _largest_divisor(Fe, 2048, LANES))

    return h.reshape(B, Lp, D)[:, N_META:L]
```

```python
import functools
import math

import jax
import jax.numpy as jnp
from jax import lax
from jax.experimental import pallas as pl
from jax.experimental.pallas import tpu as pltpu

F32 = jnp.float32
BF16 = jnp.bfloat16

N_META = 16
DEPTH = 2
LN_EPS = 1e-5
DEEPNORM_ALPHA = (2.0 * DEPTH) ** 0.25

DA_HEADS = 8
DA_HEAD_DIM = 64
DA_V_DIM = 128
SUBLN_EPS = 1e-5

RW_HEAD_DIM = 64
RW_LNX_EPS = 64e-5
RW_CHUNK = 64
RW_GROUP = 256

N_EXPERTS = 8
LANES = 128

SEQ_MULTIPLE = 1280
VMEM_LIMIT = 56 * 1024 * 1024
NEG_BIG = -1e30
LOG2E = 1.4426950408889634
ROWS = 32
MOE_TILE = 1280
MOE_ROWS = 128


def _cparams(*sem):
    return pltpu.CompilerParams(dimension_semantics=sem, vmem_limit_bytes=VMEM_LIMIT)


def _dot(a, b):
    return jnp.dot(a, b, preferred_element_type=F32)


def _dot_nt(a, b):
    return lax.dot_general(a, b, (((1,), (1,)), ((), ())), preferred_element_type=F32)


def _dot_tn(a, b):
    return lax.dot_general(a, b, (((0,), (0,)), ((), ())), preferred_element_type=F32)


def _split2(x):
    hi = x.astype(BF16)
    lo = (x - hi.astype(F32)).astype(BF16)
    return hi, lo


def _layer_norm_rows(z, g, b):
    mu = jnp.mean(z, -1, keepdims=True)
    zc = z - mu
    var = jnp.mean(zc * zc, -1, keepdims=True)
    return zc * lax.rsqrt(var + LN_EPS) * g + b


def _mm_kernel(a_ref, w_ref, o_ref):
    o_ref[...] = _dot(a_ref[...].astype(BF16), w_ref[...]).astype(o_ref.dtype)


def matmul(a, w, out_dtype, bm, bn):
    M, K = a.shape
    N = w.shape[1]
    return pl.pallas_call(
        _mm_kernel,
        grid=(N // bn, M // bm),
        in_specs=[pl.BlockSpec((bm, K), lambda j, i: (i, 0)),
                  pl.BlockSpec((K, bn), lambda j, i: (0, j))],
        out_specs=pl.BlockSpec((bm, bn), lambda j, i: (i, j)),
        out_shape=jax.ShapeDtypeStruct((M, N), out_dtype),
        compiler_params=_cparams("parallel", "parallel"),
    )(a, w)


def _mm_res_ln_kernel(a_ref, w_ref, res_ref, g_ref, b_ref, o_ref, ob_ref, acc_ref):
    k = pl.program_id(1)

    @pl.when(k == 0)
    def _():
        acc_ref[...] = jnp.zeros_like(acc_ref)

    acc_ref[...] += _dot(a_ref[...].astype(BF16), w_ref[...])

    @pl.when(k == pl.num_programs(1) - 1)
    def _():
        z = DEEPNORM_ALPHA * res_ref[...] + acc_ref[...]
        o = _layer_norm_rows(z, g_ref[...], b_ref[...])
        o_ref[...] = o
        ob_ref[...] = o.astype(BF16)


def matmul_res_ln(a, w, res, g, b, bm, bk):
    M, K = a.shape
    N = w.shape[1]
    return pl.pallas_call(
        _mm_res_ln_kernel,
        grid=(M // bm, K // bk),
        in_specs=[pl.BlockSpec((bm, bk), lambda i, k: (i, k)),
                  pl.BlockSpec((bk, N), lambda i, k: (k, 0)),
                  pl.BlockSpec((bm, N), lambda i, k: (i, 0)),
                  pl.BlockSpec((1, N), lambda i, k: (0, 0)),
                  pl.BlockSpec((1, N), lambda i, k: (0, 0))],
        out_specs=[pl.BlockSpec((bm, N), lambda i, k: (i, 0)),
                   pl.BlockSpec((bm, N), lambda i, k: (i, 0))],
        out_shape=[jax.ShapeDtypeStruct((M, N), F32), jax.ShapeDtypeStruct((M, N), BF16)],
        scratch_shapes=[pltpu.VMEM((bm, N), F32)],
        compiler_params=_cparams("parallel", "arbitrary"),
    )(a, w, res, g.reshape(1, N), b.reshape(1, N))


def _ffn_up_kernel(x_ref, wg_ref, wu_ref, o_ref):
    x = x_ref[...]
    g = _dot(x, wg_ref[...])
    u = _dot(x, wu_ref[...])
    o_ref[...] = (g * jax.nn.sigmoid(g) * u).astype(o_ref.dtype)


def ffn_up(xb, wg, wu, bm, bn):
    M, D = xb.shape
    F = wg.shape[1]
    return pl.pallas_call(
        _ffn_up_kernel,
        grid=(F // bn, M // bm),
        in_specs=[pl.BlockSpec((bm, D), lambda n, i: (i, 0)),
                  pl.BlockSpec((D, bn), lambda n, i: (0, n)),
                  pl.BlockSpec((D, bn), lambda n, i: (0, n))],
        out_specs=pl.BlockSpec((bm, bn), lambda n, i: (i, n)),
        out_shape=jax.ShapeDtypeStruct((M, F), BF16),
        compiler_params=_cparams("parallel", "parallel"),
    )(xb, wg, wu)


def _attn_kernel(slopes_ref, q_ref, k_ref, v_ref, lq1_ref, lk1_ref, lq2_ref, lk2_ref, sg_ref,
                 o_ref, va_ref, qm_ref, s0_ref, s1_ref, p0_ref, p1_ref, m_ref, al0_ref, al1_ref,
                 sh0_ref, sh1_ref, cb0_ref, cb1_ref, acc_ref,
                 *, bq, bk, nk, seq_len, lam_init):
    hd = DA_HEAD_DIM
    s_ref, p_ref, al_ref = (s0_ref, s1_ref), (p0_ref, p1_ref), (al0_ref, al1_ref)
    sh_ref, cb_ref = (sh0_ref, sh1_ref), (cb0_ref, cb1_ref)
    h = pl.program_id(1)
    qi = pl.program_id(2)
    cl = slopes_ref[h] * LOG2E
    ncc = bk // LANES

    @pl.when(qi == 0)
    def _():
        def build(jb, carry):
            off = pl.multiple_of(jb * bk, bk)
            lane = lax.broadcasted_iota(jnp.int32, (bk, DA_V_DIM), 1)
            va_ref[pl.ds(off, bk), 0:DA_V_DIM] = v_ref[pl.ds(off, bk), :]
            va_ref[pl.ds(off, bk), DA_V_DIM:] = jnp.where(lane == 0, 1.0, 0.0).astype(BF16)
            return carry
        lax.fori_loop(0, nk, build, 0)

    q = q_ref[...].astype(F32) * (hd ** -0.5 * LOG2E)
    lane = lax.broadcasted_iota(jnp.int32, (bq, 2 * hd), 1)
    qm_ref[0] = jnp.where(lane < hd, q, 0.0).astype(BF16)
    qm_ref[1] = jnp.where(lane >= hd, q, 0.0).astype(BF16)

    m_ref[...] = jnp.full(m_ref.shape, NEG_BIG, F32)
    acc_ref[...] = jnp.zeros(acc_ref.shape, F32)
    q0 = qi * bq
    n_left = q0 // bk
    first_right = (q0 + bq + bk - 1) // bk
    n_over = first_right - n_left
    n_other = nk - n_over

    def bias_pass(slot, off):
        for r in range(bq // ROWS):
            rows = slice(r * ROWS, (r + 1) * ROWS)
            qpos = q0 + r * ROWS + lax.broadcasted_iota(jnp.int32, (ROWS, LANES), 0)
            kpos = off + lax.broadcasted_iota(jnp.int32, (ROWS, LANES), 1)
            for cc in range(ncc):
                cols = slice(cc * LANES, (cc + 1) * LANES)
                bias = cl * jnp.abs(qpos - (kpos + cc * LANES)).astype(F32)
                for c in range(2):
                    s_ref[slot][c, rows, cols] = s_ref[slot][c, rows, cols] - bias

    def col_bias(slot, off, side):
        jr = lax.broadcasted_iota(jnp.int32, (8, bk), 1)
        cb = jnp.where(off + jr < seq_len, 0.0, -NEG_BIG)
        if side is not None:
            cb = cb + cl * jnp.where(side == 1, jr, bk - 1 - jr).astype(F32)
        cb_ref[slot][...] = cb

    def max_pass(slot, row_slope, row_const):
        for r in range(bq // ROWS):
            rows = slice(r * ROWS, (r + 1) * ROWS)
            ir = (r * ROWS + lax.broadcasted_iota(jnp.int32, (ROWS, LANES), 0)).astype(F32)
            roff = row_slope * ir - row_const
            for c in range(2):
                mx = s_ref[slot][c, rows, 0:LANES] - cb_ref[slot][0:1, 0:LANES]
                for cc in range(1, ncc):
                    cols = slice(cc * LANES, (cc + 1) * LANES)
                    mx = jnp.maximum(mx, s_ref[slot][c, rows, cols] - cb_ref[slot][0:1, cols])
                mx = jnp.max(mx, axis=-1, keepdims=True)
                m_old = m_ref[c, rows, :]
                m_new = jnp.maximum(m_old, mx + roff)
                m_ref[c, rows, :] = m_new
                al_ref[slot][c, rows, :] = jnp.exp2(m_old - m_new)
                sh_ref[slot][c, rows, :] = m_new - roff

    def exp_pass(slot):
        for r in range(bq // ROWS):
            rows = slice(r * ROWS, (r + 1) * ROWS)
            for c in range(2):
                shift = sh_ref[slot][c, rows, :]
                for cc in range(ncc):
                    cols = slice(cc * LANES, (cc + 1) * LANES)
                    x = s_ref[slot][c, rows, cols] - cb_ref[slot][0:1, cols] - shift
                    p_ref[slot][c, rows, cols] = jnp.exp2(x.astype(BF16))

    def rescale(slot):
        for c in range(2):
            al = al_ref[slot][c]
            acc_ref[c] = jnp.concatenate([al, al], axis=1) * acc_ref[c]

    def pv_block(slot, off):
        vb = va_ref[pl.ds(off, bk), :]
        for c in range(2):
            acc_ref[c] += _dot(p_ref[slot][c], vb)

    def over_body(j, carry):
        off = pl.multiple_of(j * bk, bk)
        for c in range(2):
            s_ref[0][c] = _dot_nt(qm_ref[c], k_ref[pl.ds(off, bk), :])
        bias_pass(0, off)
        col_bias(0, off, None)
        max_pass(0, 0.0, 0.0)
        exp_pass(0)
        rescale(0)
        pv_block(0, off)
        return carry

    lax.fori_loop(n_left, first_right, over_body, 0)

    def block_of(t):
        t = jnp.clip(t, 0, jnp.maximum(n_other - 1, 0))
        j = jnp.minimum(jnp.where(t < n_left, t, t + n_over), nk - 1)
        off = pl.multiple_of(j * bk, bk)
        side = jnp.where(j < n_left, 0, 1)
        return off, side

    def qk_max_block(slot, t):
        off, side = block_of(t)
        for c in range(2):
            s_ref[slot][c] = _dot_nt(qm_ref[c], k_ref[pl.ds(off, bk), :])
        col_bias(slot, off, side)
        gap = jnp.where(side == 1, off - q0, q0 - off - (bk - 1)).astype(F32)
        max_pass(slot, jnp.where(side == 1, cl, -cl), cl * gap)

    @pl.when(n_other > 0)
    def _():
        qk_max_block(0, 0)

    def step(t, slot):
        qk_max_block(1 - slot, t + 1)
        exp_pass(slot)
        rescale(slot)
        pv_block(slot, block_of(t)[0])

    def pair(u, carry):
        step(2 * u, 0)
        step(2 * u + 1, 1)
        return carry

    lax.fori_loop(0, n_other // 2, pair, 0)

    @pl.when(n_other % 2 == 1)
    def _():
        step(n_other - 1, 0)

    lam = (jnp.exp(jnp.sum(lq1_ref[...] * lk1_ref[...], axis=-1, keepdims=True))
           - jnp.exp(jnp.sum(lq2_ref[...] * lk2_ref[...], axis=-1, keepdims=True)) + lam_init)
    l1 = acc_ref[0, :, DA_V_DIM:DA_V_DIM + 1]
    l2 = acc_ref[1, :, DA_V_DIM:DA_V_DIM + 1]
    o = acc_ref[0, :, 0:DA_V_DIM] / l1 - lam * (acc_ref[1, :, 0:DA_V_DIM] / l2)
    o = o * lax.rsqrt(jnp.mean(o * o, -1, keepdims=True) + SUBLN_EPS) * sg_ref[...] * (1.0 - lam_init)
    o_ref[...] = o.astype(o_ref.dtype)


def diff_attention_core(qkv, lq1, lk1, lq2, lk2, subln_g, B, Lp, seq_len, layer_idx, bq, bk):
    Tp, D3 = qkv.shape
    D = D3 // 3
    H = DA_HEADS
    nq = Lp // bq
    nk = Lp // bk
    lam_init = 0.8 - 0.6 * math.exp(-0.3 * layer_idx)
    slopes = jnp.asarray([2.0 ** (-(8.0 / H) * (i + 1)) for i in range(H)], F32)
    hd = DA_HEAD_DIM
    grid_spec = pltpu.PrefetchScalarGridSpec(
        num_scalar_prefetch=1,
        grid=(B, H, nq),
        in_specs=[pl.BlockSpec((bq, 2 * hd), lambda b, h, i, s: (b * nq + i, h)),
                  pl.BlockSpec((Lp, 2 * hd), lambda b, h, i, s: (b, H + h),
                               pipeline_mode=pl.Buffered(1)),
                  pl.BlockSpec((Lp, DA_V_DIM), lambda b, h, i, s: (b, 2 * H + h),
                               pipeline_mode=pl.Buffered(1)),
                  pl.BlockSpec((1, hd), lambda b, h, i, s: (0, 0)),
                  pl.BlockSpec((1, hd), lambda b, h, i, s: (0, 0)),
                  pl.BlockSpec((1, hd), lambda b, h, i, s: (0, 0)),
                  pl.BlockSpec((1, hd), lambda b, h, i, s: (0, 0)),
                  pl.BlockSpec((1, DA_V_DIM), lambda b, h, i, s: (0, 0))],
        out_specs=pl.BlockSpec((bq, DA_V_DIM), lambda b, h, i, s: (b * nq + i, h)),
        scratch_shapes=[pltpu.VMEM((Lp, 2 * DA_V_DIM), BF16),
                        pltpu.VMEM((2, bq, 2 * hd), BF16),
                        pltpu.VMEM((2, bq, bk), F32),
                        pltpu.VMEM((2, bq, bk), F32),
                        pltpu.VMEM((2, bq, bk), BF16),
                        pltpu.VMEM((2, bq, bk), BF16),
                        pltpu.VMEM((2, bq, LANES), F32),
                        pltpu.VMEM((2, bq, LANES), F32),
                        pltpu.VMEM((2, bq, LANES), F32),
                        pltpu.VMEM((2, bq, LANES), F32),
                        pltpu.VMEM((2, bq, LANES), F32),
                        pltpu.VMEM((8, bk), F32),
                        pltpu.VMEM((8, bk), F32),
                        pltpu.VMEM((2, bq, 2 * DA_V_DIM), F32)],
    )
    return pl.pallas_call(
        functools.partial(_attn_kernel, bq=bq, bk=bk, nk=nk, seq_len=seq_len, lam_init=lam_init),
        grid_spec=grid_spec,
        out_shape=jax.ShapeDtypeStruct((Tp, D), BF16),
        compiler_params=_cparams("parallel", "parallel", "arbitrary"),
    )(slopes, qkv, qkv, qkv, lq1.reshape(1, hd), lk1.reshape(1, hd), lq2.reshape(1, hd),
      lk2.reshape(1, hd), subln_g.reshape(1, DA_V_DIM))


def _seg_sum(x, ones_bd):
    hi, lo = _split2(x)
    return _dot(hi, ones_bd) + _dot(lo, ones_bd)


def _rw_prep_kernel(x_ref, xp_ref, xn_ref, mu_ref, wrkv_ref, w1_ref, w2_ref, a1_ref, a2_ref,
                    g1_ref, g2_ref, vec_ref, ones_ref,
                    r_ref, v_ref, kk_ref, g_ref, bonus_ref, lw_ref, kd_ref, b_ref,
                    *, bm, Lp, seq_len, lora_w, lora_a):
    i = pl.program_id(0)
    x = x_ref[...]
    row = lax.broadcasted_iota(jnp.int32, (bm, 1), 0)
    pos = (i * bm) % Lp + row
    prev_last = xp_ref[7:8, :]
    next_first = xn_ref[0:1, :]
    x_prev = jnp.where(row == 0, prev_last, pltpu.roll(x, 1, 0))
    x_next = jnp.where(row == bm - 1, next_first, pltpu.roll(x, bm - 1, 0))
    x_prev = jnp.where(pos == 0, 0.0, x_prev)
    x_next = jnp.where(pos == seq_len - 1, 0.0, x_next)
    dxp = x_prev - x
    dxn = x_next - x
    valid = pos < seq_len

    def mix(n):
        return (x + mu_ref[n:n + 1, :] * dxp + mu_ref[6 + n:7 + n, :] * dxn).astype(BF16)

    r = _dot(mix(0), wrkv_ref[0])
    k = _dot(mix(1), wrkv_ref[1])
    v = _dot(mix(2), wrkv_ref[2])

    lane = lax.broadcasted_iota(jnp.int32, (bm, 2 * lora_w), 1)
    th = jnp.tanh(_dot(mix(3), w1_ref[...]))
    lw = [_dot(jnp.where((lane >= n * lora_w) & (lane < (n + 1) * lora_w), th, 0.0).astype(BF16),
               w2_ref[...]) for n in range(2)]
    lane = lax.broadcasted_iota(jnp.int32, (bm, 2 * lora_a), 1)
    ah = _dot(mix(4), a1_ref[...])
    la = [_dot(jnp.where((lane >= n * lora_a) & (lane < (n + 1) * lora_a), ah, 0.0).astype(BF16),
               a2_ref[...]) for n in range(2)]
    g = _dot(jax.nn.sigmoid(_dot(mix(5), g1_ref[...])).astype(BF16), g2_ref[...])

    k_k = vec_ref[4:5, :]
    k_a = vec_ref[5:6, :]
    r_k = vec_ref[6:7, :]
    ones_bd = ones_ref[...]

    kkr = k * k_k
    nrm = jnp.sqrt(_seg_sum(kkr * kkr, ones_bd))
    kk = kkr / jnp.maximum(nrm, 1e-12)
    kk = jnp.where(valid, kk, 0.0)
    vz = jnp.where(valid, v, 0.0)

    kd_sum = jnp.zeros_like(k)
    for n in range(2):
        z = -(vec_ref[n:n + 1, :] + lw[n])
        softplus = jnp.maximum(z, 0.0) + jnp.log(1.0 + jnp.exp(-jnp.abs(z)))
        w_log = -softplus - 0.5
        lw_ref[n] = jnp.where(valid, -jnp.exp(w_log), 0.0)
        a = jax.nn.sigmoid(vec_ref[2 + n:3 + n, :] + la[n])
        kd = k * (1.0 + (a - 1.0) * k_a)
        kd_sum = kd_sum + kd
        kd_ref[n] = jnp.where(valid, kd, 0.0)
        b_ref[n] = kk * a

    bonus = _seg_sum(r * kd_sum * r_k, ones_bd) * v
    r_ref[...] = r
    v_ref[...] = vz
    kk_ref[...] = kk
    g_ref[...] = g
    bonus_ref[...] = bonus


def rwkv_prep(h, mu, w_rkv, w1, w2, a1, a2, g1, g2, w0, a0, k_k, k_a, r_k, Lp, seq_len, bm):
    Tp, D = h.shape
    lora_w = w1.shape[-1]
    lora_a = a1.shape[-1]
    lora_g = g1.shape[-1]
    gpad = -(-lora_g // LANES) * LANES
    mu12 = mu.reshape(12, D)
    w1c = jnp.concatenate([w1[0], w1[1]], axis=1).astype(BF16)
    w2c = jnp.concatenate([w2[0], w2[1]], axis=0).astype(BF16)
    a1c = jnp.concatenate([a1[0], a1[1]], axis=1).astype(BF16)
    a2c = jnp.concatenate([a2[0], a2[1]], axis=0).astype(BF16)
    g1p = jnp.pad(g1, ((0, 0), (0, gpad - lora_g))).astype(BF16)
    g2p = jnp.pad(g2, ((0, gpad - lora_g), (0, 0))).astype(BF16)
    vecs = jnp.concatenate([w0, a0, k_k.reshape(1, D), k_a.reshape(1, D), r_k.reshape(1, D),
                            jnp.zeros((1, D), F32)], axis=0)
    seg = jnp.arange(D) // RW_HEAD_DIM
    ones_bd = (seg[:, None] == seg[None, :]).astype(BF16)
    nb8 = bm // 8
    last8 = Tp // 8 - 1
    const2 = lambda i: (0, 0)
    const3 = lambda i: (0, 0, 0)
    row = pl.BlockSpec((bm, D), lambda i: (i, 0))
    row2 = pl.BlockSpec((2, bm, D), lambda i: (0, i, 0))
    sds = jax.ShapeDtypeStruct((Tp, D), F32)
    sds2 = jax.ShapeDtypeStruct((2, Tp, D), F32)
    return pl.pallas_call(
        functools.partial(_rw_prep_kernel, bm=bm, Lp=Lp, seq_len=seq_len, lora_w=lora_w, lora_a=lora_a),
        grid=(Tp // bm,),
        in_specs=[row,
                  pl.BlockSpec((8, D), lambda i: (jnp.maximum(i * nb8 - 1, 0), 0)),
                  pl.BlockSpec((8, D), lambda i: (jnp.minimum((i + 1) * nb8, last8), 0)),
                  pl.BlockSpec((12, D), const2),
                  pl.BlockSpec((3, D, D), const3),
                  pl.BlockSpec((D, 2 * lora_w), const2),
                  pl.BlockSpec((2 * lora_w, D), const2),
                  pl.BlockSpec((D, 2 * lora_a), const2),
                  pl.BlockSpec((2 * lora_a, D), const2),
                  pl.BlockSpec((D, gpad), const2),
                  pl.BlockSpec((gpad, D), const2),
                  pl.BlockSpec((8, D), const2),
                  pl.BlockSpec((D, D), const2)],
        out_specs=[row, row, row, row, row, row2, row2, row2],
        out_shape=[sds, sds, sds, sds, sds, sds2, sds2, sds2],
        compiler_params=_cparams("parallel"),
    )(h, h, h, mu12, w_rkv.astype(BF16), w1c, w2c, a1c, a2c, g1p, g2p, vecs, ones_bd)


def _rw_scan_kernel(r_ref, v_ref, kk_ref, lw_ref, kd_ref, b_ref, y_ref, s_ref, *, C, D):
    d = pl.program_id(1)
    c = pl.program_id(2)
    G_ = RW_GROUP
    hpg = G_ // C
    fwd = d == 0

    @pl.when(c == 0)
    def _():
        s_ref[...] = jnp.zeros_like(s_ref)

    sgn = jnp.where(fwd, 1, -1)
    ri = lax.broadcasted_iota(jnp.int32, (C, C), 0)
    ci = lax.broadcasted_iota(jnp.int32, (C, C), 1)
    tri = jnp.where((ri - ci) * sgn >= 0, 1.0, 0.0).astype(BF16)

    lw = lw_ref[...]
    hi = lw.astype(BF16)
    r1 = lw - hi.astype(F32)
    mid = r1.astype(BF16)
    lo = (r1 - mid.astype(F32)).astype(BF16)
    G = _dot(tri, hi) + _dot(tri, mid) + _dot(tri, lo)
    g_end = jnp.where(fwd, G[C - 1:C, :], G[0:1, :])
    e_pos = jnp.exp(G)
    e_neg = jnp.exp(-G)
    e_exc = jnp.exp(G - lw)
    e_rem = jnp.exp(g_end - G)
    kk = kk_ref[...]
    b = b_ref[...]
    kd = kd_ref[...]
    At = (-kk * e_exc).astype(BF16)
    Bt = (b * e_neg).astype(BF16)
    Kt = (kd * e_neg).astype(BF16)
    Rt = (r_ref[...] * e_pos).astype(BF16)
    Bh = (b * e_rem).astype(BF16)
    Kh = (kd * e_rem).astype(BF16)
    Vb = v_ref[...].astype(BF16)
    gam = jnp.exp(g_end)

    R = lax.broadcasted_iota(jnp.int32, (G_, G_), 0)
    Cc = lax.broadcasted_iota(jnp.int32, (G_, G_), 1)
    blk = (R // C) == (Cc // RW_HEAD_DIM)
    dt = jnp.where((R // C) == (Cc // C), (R % C - Cc % C) * sgn, -1)
    strict = dt > 0
    incl = dt >= 0
    eye = jnp.where(R == Cc, 1.0, 0.0)
    zero_b = jnp.zeros((G_, G_), BF16)

    def expand(x, q):
        xq = x[:, q * G_:(q + 1) * G_]
        return jnp.where(blk, jnp.concatenate([xq] * hpg, axis=0), zero_b)

    groups = range(D // G_)
    Ae = [expand(At, q) for q in groups]
    Be = [expand(Bt, q) for q in groups]
    Ke = [expand(Kt, q) for q in groups]
    Re = [expand(Rt, q) for q in groups]
    Ve = [expand(Vb, q) for q in groups]
    Sb = [s_ref[q].astype(BF16) for q in groups]
    N = [jnp.where(strict, _dot_nt(Ae[q], Be[q]), 0.0) for q in groups]
    P = [eye + N[q] for q in groups]
    Xb = [N[q].astype(BF16) for q in groups]
    Aak = [jnp.where(strict, _dot_nt(Ae[q], Ke[q]), 0.0).astype(BF16) for q in groups]
    rhs = [(_dot_nt(Ae[q], Sb[q]) + _dot(Aak[q], Ve[q])).astype(BF16) for q in groups]
    p = 1
    while 2 * p < C:
        Xb = [_dot(Xb[q], Xb[q]).astype(BF16) for q in groups]
        P = [P[q] + _dot(P[q].astype(BF16), Xb[q]) for q in groups]
        p *= 2
    Ub = [_dot(P[q].astype(BF16), rhs[q]).astype(BF16) for q in groups]
    Arb = [jnp.where(incl, _dot_nt(Re[q], Be[q]), 0.0).astype(BF16) for q in groups]
    Ark = [jnp.where(incl, _dot_nt(Re[q], Ke[q]), 0.0).astype(BF16) for q in groups]
    for q in groups:
        Y = _dot_nt(Re[q], Sb[q]) + _dot(Arb[q], Ub[q]) + _dot(Ark[q], Ve[q])
        yq = Y[0:C, :]
        for hb in range(1, hpg):
            yq = yq + Y[hb * C:(hb + 1) * C, :]
        y_ref[:, q * G_:(q + 1) * G_] = yq
    for q in groups:
        s_ref[q] = (s_ref[q] * gam[:, q * G_:(q + 1) * G_] + _dot_tn(Ub[q], expand(Bh, q))
                    + _dot_tn(Ve[q], expand(Kh, q)))


def rwkv_scan(r, v, kk, lw, kd, b, B, Lp):
    Tp, D = r.shape
    C = RW_CHUNK
    nch = Lp // C

    def rb(bi, d, c):
        return bi * nch + jnp.where(d == 0, c, nch - 1 - c)

    shared = pl.BlockSpec((C, D), lambda bi, d, c: (rb(bi, d, c), 0))
    perdir = pl.BlockSpec((None, C, D), lambda bi, d, c: (d, rb(bi, d, c), 0))
    return pl.pallas_call(
        functools.partial(_rw_scan_kernel, C=C, D=D),
        grid=(B, 2, nch),
        in_specs=[shared, shared, shared, perdir, perdir, perdir],
        out_specs=perdir,
        out_shape=jax.ShapeDtypeStruct((2, Tp, D), F32),
        scratch_shapes=[pltpu.VMEM((D // RW_GROUP, RW_GROUP, RW_GROUP), F32)],
        compiler_params=_cparams("parallel", "parallel", "arbitrary"),
    )(r, v, kk, lw, kd, b)


def _rw_post_kernel(yf_ref, yb_ref, g_ref, bonus_ref, res_ref, vec_ref, avg_ref, wo_ref,
                    o_ref, ob_ref):
    avg = avg_ref[...]
    y = yf_ref[...] + yb_ref[...]
    ym = _seg_sum(y, avg)
    yc = y - ym
    yv = _seg_sum(yc * yc, avg)
    yn = yc * lax.rsqrt(yv + RW_LNX_EPS) * vec_ref[0:1, :] + vec_ref[1:2, :]
    out = ((yn + bonus_ref[...]) * g_ref[...]).astype(BF16)
    z = DEEPNORM_ALPHA * res_ref[...] + _dot(out, wo_ref[...])
    o = _layer_norm_rows(z, vec_ref[2:3, :], vec_ref[3:4, :])
    o_ref[...] = o
    ob_ref[...] = o.astype(BF16)


def rwkv_post(y2, g, bonus, res, lnx_g, lnx_b, w_o, ln_g, ln_b, bm):
    Tp, D = res.shape
    vecs = jnp.concatenate([lnx_g.reshape(1, D), lnx_b.reshape(1, D), ln_g.reshape(1, D),
                            ln_b.reshape(1, D), jnp.zeros((4, D), F32)], axis=0)
    seg = jnp.arange(D) // RW_HEAD_DIM
    avg = ((seg[:, None] == seg[None, :]).astype(F32) / RW_HEAD_DIM).astype(BF16)
    row = pl.BlockSpec((bm, D), lambda i: (i, 0))
    return pl.pallas_call(
        _rw_post_kernel,
        grid=(Tp // bm,),
        in_specs=[pl.BlockSpec((None, bm, D), lambda i: (0, i, 0)),
                  pl.BlockSpec((None, bm, D), lambda i: (1, i, 0)),
                  row, row, row,
                  pl.BlockSpec((8, D), lambda i: (0, 0)),
                  pl.BlockSpec((D, D), lambda i: (0, 0)),
                  pl.BlockSpec((D, D), lambda i: (0, 0))],
        out_specs=[row, row],
        out_shape=[jax.ShapeDtypeStruct((Tp, D), F32), jax.ShapeDtypeStruct((Tp, D), BF16)],
        compiler_params=_cparams("parallel"),
    )(y2, y2, g, bonus, res, vecs, avg, w_o.astype(BF16))


def _router_kernel(x_ref, wh_ref, wl_ref, b_ref, grow_ref, dcol_ref, drow_ref, cnt_ref):
    x = x_ref[...]
    xh, xl = _split2(x)
    wh = wh_ref[...]
    logits = _dot(xh, wh) + _dot(xh, wl_ref[...]) + _dot(xl, wh) + b_ref[...]
    lane = lax.broadcasted_iota(jnp.int32, logits.shape, 1)
    mx = jnp.max(logits, axis=-1, keepdims=True)
    ex = jnp.exp(logits - mx)
    probs = ex / jnp.sum(ex, axis=-1, keepdims=True)
    big = jnp.int32(LANES)
    m1 = jnp.max(probs, axis=-1, keepdims=True)
    i1 = jnp.min(jnp.where(probs == m1, lane, big), axis=-1, keepdims=True)
    sel1 = lane == i1
    rest = jnp.where(sel1 | (lane >= N_EXPERTS), -1.0, probs)
    m2 = jnp.max(rest, axis=-1, keepdims=True)
    i2 = jnp.min(jnp.where(rest == m2, lane, big), axis=-1, keepdims=True)
    sel2 = lane == i2
    tot = m1 + m2
    gates = jnp.where(sel1, m1 / tot, 0.0) + jnp.where(sel2, m2 / tot, 0.0)
    grow_ref[...] = gates.T

    tm = gates.shape[0]
    sel = jnp.where(gates > 0.0, 1.0, 0.0).astype(BF16)
    ti = lax.broadcasted_iota(jnp.int32, (tm, tm), 0)
    tj = lax.broadcasted_iota(jnp.int32, (tm, tm), 1)
    slot_col = _dot(jnp.where(tj < ti, 1.0, 0.0).astype(BF16), sel)
    dcol_ref[...] = jnp.where(gates > 0.0, slot_col, -1.0)
    excl = _dot_tn(sel, jnp.where(ti < tj, 1.0, 0.0).astype(BF16))
    incl = _dot_tn(sel, jnp.where(ti <= tj, 1.0, 0.0).astype(BF16))
    drow_ref[...] = jnp.where(incl - excl > 0.5, excl, -1.0)
    cnt = jnp.sum(jnp.where(gates > 0.0, 1.0, 0.0), axis=0, keepdims=True)
    cnt_ref[...] = jnp.broadcast_to(cnt, cnt_ref.shape).astype(jnp.int32)


def moe_router(h, w_router, b_router, tm):
    Tp, D = h.shape
    E = w_router.shape[1]
    nt = Tp // tm
    wp = jnp.pad(w_router, ((0, 0), (0, LANES - E)))
    wh = wp.astype(BF16)
    wl = (wp - wh.astype(F32)).astype(BF16)
    bp = jnp.concatenate([b_router, jnp.full((LANES - E,), NEG_BIG, F32)]).reshape(1, LANES)
    return pl.pallas_call(
        _router_kernel,
        grid=(nt,),
        in_specs=[pl.BlockSpec((tm, D), lambda i: (i, 0)),
                  pl.BlockSpec((D, LANES), lambda i: (0, 0)),
                  pl.BlockSpec((D, LANES), lambda i: (0, 0)),
                  pl.BlockSpec((1, LANES), lambda i: (0, 0))],
        out_specs=[pl.BlockSpec((None, LANES, tm), lambda i: (i, 0, 0)),
                   pl.BlockSpec((tm, LANES), lambda i: (i, 0)),
                   pl.BlockSpec((None, LANES, tm), lambda i: (i, 0, 0)),
                   pl.BlockSpec((None, 8, LANES), lambda i: (i, 0, 0))],
        out_shape=[jax.ShapeDtypeStruct((nt, LANES, tm), F32),
                   jax.ShapeDtypeStruct((Tp, LANES), F32),
                   jax.ShapeDtypeStruct((nt, LANES, tm), F32),
                   jax.ShapeDtypeStruct((nt, 8, LANES), jnp.int32)],
        compiler_params=_cparams("parallel"),
    )(h, wh, wl, bp)


def _moe_kernel(cnt_ref, x_ref, grow_ref, dcol_ref, drow_ref, wg_ref, wu_ref, wd_ref,
                y_ref, xg_ref, gc_ref, yo_ref, *, tm, n_f):
    i = pl.program_id(0)
    e = pl.program_id(1)
    f = pl.program_id(2)
    nb = (cnt_ref[i, e] + MOE_ROWS - 1) // MOE_ROWS
    npair = (nb + 1) // 2
    PAIR = 2 * MOE_ROWS

    def rows_of(blk):
        return pl.ds(pl.multiple_of(blk * MOE_ROWS, MOE_ROWS), MOE_ROWS)

    def pair_rows(pr):
        return pl.ds(pl.multiple_of(pr * PAIR, PAIR), PAIR)

    @pl.when((e == 0) & (f == 0))
    def _():
        y_ref[...] = jnp.zeros_like(y_ref)

    @pl.when(f == 0)
    def _():
        x = x_ref[...]
        slot = drow_ref[pl.ds(e, 1), :]
        gate = grow_ref[pl.ds(e, 1), :]

        def gather(pr, carry):
            want = (pr * PAIR + lax.broadcasted_iota(jnp.int32, (PAIR, tm), 0)).astype(F32)
            hit = slot == want
            xg_ref[pair_rows(pr), :] = _dot(jnp.where(hit, 1.0, 0.0).astype(BF16), x).astype(BF16)
            gsel = jnp.sum(jnp.where(hit, gate, 0.0), axis=-1, keepdims=True)
            gc_ref[pair_rows(pr), :] = jnp.broadcast_to(gsel, (PAIR, LANES))
            yo_ref[pair_rows(pr), :] = jnp.zeros((PAIR, yo_ref.shape[1]), F32)
            return carry
        lax.fori_loop(0, npair, gather, 0)

    def ffn(blk, carry):
        xb = xg_ref[rows_of(blk), :]
        g = _dot(xb, wg_ref[...])
        u = _dot(xb, wu_ref[...])
        act = (g * jax.nn.sigmoid(g) * u * gc_ref[rows_of(blk), 0:1]).astype(BF16)
        yo_ref[rows_of(blk), :] += _dot(act, wd_ref[...])
        return carry
    lax.fori_loop(0, nb, ffn, 0)

    @pl.when(f == n_f - 1)
    def _():
        lane_t = lax.broadcasted_iota(jnp.int32, (tm, LANES), 1)
        slot = jnp.sum(jnp.where(lane_t == e, dcol_ref[...], 0.0), axis=-1, keepdims=True)

        def scatter(pr, carry):
            have = (pr * PAIR + lax.broadcasted_iota(jnp.int32, (tm, PAIR), 1)).astype(F32)
            put = jnp.where(slot == have, 1.0, 0.0).astype(BF16)
            yh, yl = _split2(yo_ref[pair_rows(pr), :])
            y_ref[...] += _dot(put, yh) + _dot(put, yl)
            return carry
        lax.fori_loop(0, npair, scatter, 0)


def moe_experts(xb, grow, dcol, drow, cnt, wg, wu, wd, tm, fc):
    Tp, D = xb.shape
    E, _, Fe = wg.shape
    n_f = Fe // fc
    nt = Tp // tm
    grid_spec = pltpu.PrefetchScalarGridSpec(
        num_scalar_prefetch=1,
        grid=(nt, E, n_f),
        in_specs=[pl.BlockSpec((tm, D), lambda i, e, f, c: (i, 0)),
                  pl.BlockSpec((None, LANES, tm), lambda i, e, f, c: (i, 0, 0)),
                  pl.BlockSpec((tm, LANES), lambda i, e, f, c: (i, 0)),
                  pl.BlockSpec((None, LANES, tm), lambda i, e, f, c: (i, 0, 0)),
                  pl.BlockSpec((None, D, fc), lambda i, e, f, c: (e, 0, f)),
                  pl.BlockSpec((None, D, fc), lambda i, e, f, c: (e, 0, f)),
                  pl.BlockSpec((None, fc, D), lambda i, e, f, c: (e, f, 0))],
        out_specs=pl.BlockSpec((tm, D), lambda i, e, f, c: (i, 0)),
        scratch_shapes=[pltpu.VMEM((tm + 2 * MOE_ROWS, D), BF16),
                        pltpu.VMEM((tm + 2 * MOE_ROWS, LANES), F32),
                        pltpu.VMEM((tm + 2 * MOE_ROWS, D), F32)],
    )
    return pl.pallas_call(
        functools.partial(_moe_kernel, tm=tm, n_f=n_f),
        grid_spec=grid_spec,
        out_shape=jax.ShapeDtypeStruct((Tp, D), F32),
        compiler_params=_cparams("parallel", "arbitrary", "arbitrary"),
    )(cnt, xb, grow, dcol, drow, wg, wu, wd)


def _res_ln_kernel(y_ref, res_ref, g_ref, b_ref, o_ref):
    o_ref[...] = _layer_norm_rows(DEEPNORM_ALPHA * res_ref[...] + y_ref[...], g_ref[...], b_ref[...])


def res_ln(y, res, g, b, bm):
    M, N = y.shape
    row = pl.BlockSpec((bm, N), lambda i: (i, 0))
    vec = pl.BlockSpec((1, N), lambda i: (0, 0))
    return pl.pallas_call(
        _res_ln_kernel,
        grid=(M // bm,),
        in_specs=[row, row, vec, vec],
        out_specs=row,
        out_shape=jax.ShapeDtypeStruct((M, N), F32),
        compiler_params=_cparams("parallel"),
    )(y, res, g.reshape(1, N), b.reshape(1, N))


def _largest_divisor(n, cap, mult):
    best = mult
    for c in range(mult, cap + 1, mult):
        if n % c == 0:
            best = c
    return best


def kernel(x, meta, ln_g, ln_b, attn_w_in, attn_w_o, attn_lam_q1, attn_lam_k1, attn_lam_q2, attn_lam_k2, attn_subln_g, ffn_w_gate, ffn_w_up, ffn_w_down, rw_mu, rw_w_rkv, rw_w0, rw_w1, rw_w2, rw_a0, rw_a1, rw_a2, rw_g1, rw_g2, rw_k_k, rw_k_a, rw_r_k, rw_lnx_g, rw_lnx_b, rw_w_o, moe_w_router, moe_b_router, moe_w_gate, moe_w_up, moe_w_down):
    B, S, D = x.shape
    L = S + N_META
    Lp = -(-L // SEQ_MULTIPLE) * SEQ_MULTIPLE
    Tp = B * Lp
    BM = 640
    BM_RW = 256

    h = jnp.concatenate([jnp.broadcast_to(meta[None].astype(x.dtype), (B, N_META, D)), x,
                         jnp.zeros((B, Lp - L, D), x.dtype)], axis=1).reshape(Tp, D)

    qkv = matmul(h, attn_w_in[0].astype(BF16), BF16, BM, 1024)
    o = diff_attention_core(qkv, attn_lam_q1[0], attn_lam_k1[0], attn_lam_q2[0], attn_lam_k2[0],
                            attn_subln_g[0], B, Lp, L, 0, bq=640, bk=1280)
    h, hb = matmul_res_ln(o, attn_w_o[0].astype(BF16), h, ln_g[0, 0], ln_b[0, 0], BM, D)
    F = ffn_w_gate.shape[-1]
    act = ffn_up(hb, ffn_w_gate[0].astype(BF16), ffn_w_up[0].astype(BF16), BM,
                 _largest_divisor(F, 1792, LANES))
    h, hb = matmul_res_ln(act, ffn_w_down[0].astype(BF16), h, ln_g[0, 1], ln_b[0, 1], BM,
                          _largest_divisor(F, 2048, LANES))

    r, v, kk, g, bonus, lw, kd, b = rwkv_prep(
        h, rw_mu[0], rw_w_rkv[0], rw_w1[0], rw_w2[0], rw_a1[0], rw_a2[0], rw_g1[0], rw_g2[0],
        rw_w0[0], rw_a0[0], rw_k_k[0], rw_k_a[0], rw_r_k[0], Lp, L, BM_RW)
    y2 = rwkv_scan(r, v, kk, lw, kd, b, B, Lp)
    h, hb = rwkv_post(y2, g, bonus, h, rw_lnx_g[0], rw_lnx_b[0], rw_w_o[0], ln_g[1, 0], ln_b[1, 0],
                      BM_RW)

    grow, dcol, drow, cnt = moe_router(h, moe_w_router[0], moe_b_router[0], MOE_TILE)
    Fe = moe_w_gate.shape[-1]
    E = moe_w_gate.shape[1]
    y = moe_experts(hb, grow, dcol, drow, cnt[:, 0, :E], moe_w_gate[0].astype(BF16),
                    moe_w_up[0].astype(BF16), moe_w_down[0].astype(BF16), MOE_TILE,
                    _largest_divisor(Fe, 1792, LANES))
    h = res_ln(y, h, ln_g[1, 1], ln_b[1, 1], BM)

    return h.reshape(B, Lp, D)[:, N_META:L]
```

```python
import functools
import math

import jax
import jax.numpy as jnp
from jax import lax
from jax.experimental import pallas as pl
from jax.experimental.pallas import tpu as pltpu

F32 = jnp.float32
BF16 = jnp.bfloat16

N_META = 16
DEPTH = 2
LN_EPS = 1e-5
DEEPNORM_ALPHA = (2.0 * DEPTH) ** 0.25

DA_HEADS = 8
DA_HEAD_DIM = 64
DA_V_DIM = 128
SUBLN_EPS = 1e-5

RW_HEAD_DIM = 64
RW_LNX_EPS = 64e-5
RW_CHUNK = 64
RW_GROUP = 256

N_EXPERTS = 8
LANES = 128

SEQ_MULTIPLE = 1280
VMEM_LIMIT = 56 * 1024 * 1024
NEG_BIG = -1e30
LOG2E = 1.4426950408889634
ROWS = 32
STEPS_PER_TRIP = 4
MOE_TILE = 1280
MOE_ROWS = 128


def _cparams(*sem):
    return pltpu.CompilerParams(dimension_semantics=sem, vmem_limit_bytes=VMEM_LIMIT)


def _dot(a, b):
    return jnp.dot(a, b, preferred_element_type=F32)


def _dot_nt(a, b):
    return lax.dot_general(a, b, (((1,), (1,)), ((), ())), preferred_element_type=F32)


def _dot_tn(a, b):
    return lax.dot_general(a, b, (((0,), (0,)), ((), ())), preferred_element_type=F32)


def _split2(x):
    hi = x.astype(BF16)
    lo = (x - hi.astype(F32)).astype(BF16)
    return hi, lo


def _layer_norm_rows(z, g, b):
    mu = jnp.mean(z, -1, keepdims=True)
    zc = z - mu
    var = jnp.mean(zc * zc, -1, keepdims=True)
    return zc * lax.rsqrt(var + LN_EPS) * g + b


def _mm_kernel(a_ref, w_ref, o_ref):
    o_ref[...] = _dot(a_ref[...].astype(BF16), w_ref[...]).astype(o_ref.dtype)


def matmul(a, w, out_dtype, bm, bn):
    M, K = a.shape
    N = w.shape[1]
    return pl.pallas_call(
        _mm_kernel,
        grid=(N // bn, M // bm),
        in_specs=[pl.BlockSpec((bm, K), lambda j, i: (i, 0)),
                  pl.BlockSpec((K, bn), lambda j, i: (0, j))],
        out_specs=pl.BlockSpec((bm, bn), lambda j, i: (i, j)),
        out_shape=jax.ShapeDtypeStruct((M, N), out_dtype),
        compiler_params=_cparams("parallel", "parallel"),
    )(a, w)


def _mm_res_ln_kernel(a_ref, w_ref, res_ref, g_ref, b_ref, o_ref, ob_ref, acc_ref):
    k = pl.program_id(1)

    @pl.when(k == 0)
    def _():
        acc_ref[...] = jnp.zeros_like(acc_ref)

    acc_ref[...] += _dot(a_ref[...].astype(BF16), w_ref[...])

    @pl.when(k == pl.num_programs(1) - 1)
    def _():
        z = DEEPNORM_ALPHA * res_ref[...] + acc_ref[...]
        o = _layer_norm_rows(z, g_ref[...], b_ref[...])
        o_ref[...] = o
        ob_ref[...] = o.astype(BF16)


def matmul_res_ln(a, w, res, g, b, bm, bk):
    M, K = a.shape
    N = w.shape[1]
    return pl.pallas_call(
        _mm_res_ln_kernel,
        grid=(M // bm, K // bk),
        in_specs=[pl.BlockSpec((bm, bk), lambda i, k: (i, k)),
                  pl.BlockSpec((bk, N), lambda i, k: (k, 0)),
                  pl.BlockSpec((bm, N), lambda i, k: (i, 0)),
                  pl.BlockSpec((1, N), lambda i, k: (0, 0)),
                  pl.BlockSpec((1, N), lambda i, k: (0, 0))],
        out_specs=[pl.BlockSpec((bm, N), lambda i, k: (i, 0)),
                   pl.BlockSpec((bm, N), lambda i, k: (i, 0))],
        out_shape=[jax.ShapeDtypeStruct((M, N), F32), jax.ShapeDtypeStruct((M, N), BF16)],
        scratch_shapes=[pltpu.VMEM((bm, N), F32)],
        compiler_params=_cparams("parallel", "arbitrary"),
    )(a, w, res, g.reshape(1, N), b.reshape(1, N))


def _ffn_up_kernel(x_ref, wg_ref, wu_ref, o_ref):
    x = x_ref[...]
    g = _dot(x, wg_ref[...])
    u = _dot(x, wu_ref[...])
    o_ref[...] = (g * jax.nn.sigmoid(g) * u).astype(o_ref.dtype)


def ffn_up(xb, wg, wu, bm, bn):
    M, D = xb.shape
    F = wg.shape[1]
    return pl.pallas_call(
        _ffn_up_kernel,
        grid=(F // bn, M // bm),
        in_specs=[pl.BlockSpec((bm, D), lambda n, i: (i, 0)),
                  pl.BlockSpec((D, bn), lambda n, i: (0, n)),
                  pl.BlockSpec((D, bn), lambda n, i: (0, n))],
        out_specs=pl.BlockSpec((bm, bn), lambda n, i: (i, n)),
        out_shape=jax.ShapeDtypeStruct((M, F), BF16),
        compiler_params=_cparams("parallel", "parallel"),
    )(xb, wg, wu)


def _attn_kernel(slopes_ref, q_ref, k_ref, v_ref, lq1_ref, lk1_ref, lq2_ref, lk2_ref, sg_ref,
                 o_ref, va_ref, qm_ref, s0_ref, s1_ref, p0_ref, p1_ref, m_ref, al0_ref, al1_ref,
                 sh0_ref, sh1_ref, cb0_ref, cb1_ref, acc_ref,
                 *, bq, bk, nk, seq_len, lam_init):
    hd = DA_HEAD_DIM
    s_ref, p_ref, al_ref = (s0_ref, s1_ref), (p0_ref, p1_ref), (al0_ref, al1_ref)
    sh_ref, cb_ref = (sh0_ref, sh1_ref), (cb0_ref, cb1_ref)
    h = pl.program_id(1)
    qi = pl.program_id(2)
    cl = slopes_ref[h] * LOG2E
    ncc = bk // LANES

    @pl.when(qi == 0)
    def _():
        def build(jb, carry):
            off = pl.multiple_of(jb * bk, bk)
            lane = lax.broadcasted_iota(jnp.int32, (bk, DA_V_DIM), 1)
            va_ref[pl.ds(off, bk), 0:DA_V_DIM] = v_ref[pl.ds(off, bk), :]
            va_ref[pl.ds(off, bk), DA_V_DIM:] = jnp.where(lane == 0, 1.0, 0.0).astype(BF16)
            return carry
        lax.fori_loop(0, nk, build, 0)

    q = q_ref[...].astype(F32) * (hd ** -0.5 * LOG2E)
    lane = lax.broadcasted_iota(jnp.int32, (bq, 2 * hd), 1)
    qm_ref[0] = jnp.where(lane < hd, q, 0.0).astype(BF16)
    qm_ref[1] = jnp.where(lane >= hd, q, 0.0).astype(BF16)

    m_ref[...] = jnp.full(m_ref.shape, NEG_BIG, F32)
    acc_ref[...] = jnp.zeros(acc_ref.shape, F32)
    q0 = qi * bq
    n_left = q0 // bk
    first_right = (q0 + bq + bk - 1) // bk
    n_over = first_right - n_left
    n_other = nk - n_over

    def bias_pass(slot, off):
        for r in range(bq // ROWS):
            rows = slice(r * ROWS, (r + 1) * ROWS)
            qpos = q0 + r * ROWS + lax.broadcasted_iota(jnp.int32, (ROWS, LANES), 0)
            kpos = off + lax.broadcasted_iota(jnp.int32, (ROWS, LANES), 1)
            for cc in range(ncc):
                cols = slice(cc * LANES, (cc + 1) * LANES)
                bias = cl * jnp.abs(qpos - (kpos + cc * LANES)).astype(F32)
                for c in range(2):
                    s_ref[slot][c, rows, cols] = s_ref[slot][c, rows, cols] - bias

    def col_bias(slot, off, side):
        jr = lax.broadcasted_iota(jnp.int32, (8, bk), 1)
        cb = jnp.where(off + jr < seq_len, 0.0, -NEG_BIG)
        if side is not None:
            cb = cb + cl * jnp.where(side == 1, jr, bk - 1 - jr).astype(F32)
        cb_ref[slot][...] = cb

    def max_pass(slot, row_slope, row_const):
        for r in range(bq // ROWS):
            rows = slice(r * ROWS, (r + 1) * ROWS)
            ir = (r * ROWS + lax.broadcasted_iota(jnp.int32, (ROWS, LANES), 0)).astype(F32)
            roff = row_slope * ir - row_const
            for c in range(2):
                mx = s_ref[slot][c, rows, 0:LANES] - cb_ref[slot][0:1, 0:LANES]
                for cc in range(1, ncc):
                    cols = slice(cc * LANES, (cc + 1) * LANES)
                    mx = jnp.maximum(mx, s_ref[slot][c, rows, cols] - cb_ref[slot][0:1, cols])
                mx = jnp.max(mx, axis=-1, keepdims=True)
                m_old = m_ref[c, rows, :]
                m_new = jnp.maximum(m_old, mx + roff)
                m_ref[c, rows, :] = m_new
                al_ref[slot][c, rows, :] = jnp.exp2(m_old - m_new)
                sh_ref[slot][c, rows, :] = m_new - roff

    def exp_pass(slot):
        for r in range(bq // ROWS):
            rows = slice(r * ROWS, (r + 1) * ROWS)
            for c in range(2):
                shift = sh_ref[slot][c, rows, :]
                for cc in range(ncc):
                    cols = slice(cc * LANES, (cc + 1) * LANES)
                    x = s_ref[slot][c, rows, cols] - cb_ref[slot][0:1, cols] - shift
                    p_ref[slot][c, rows, cols] = jnp.exp2(x.astype(BF16))

    def rescale(slot):
        for c in range(2):
            al = al_ref[slot][c]
            acc_ref[c] = jnp.concatenate([al, al], axis=1) * acc_ref[c]

    def pv_block(slot, off):
        vb = va_ref[pl.ds(off, bk), :]
        for c in range(2):
            acc_ref[c] += _dot(p_ref[slot][c], vb)

    def block_of(t):
        t = jnp.clip(t, 0, jnp.maximum(n_other - 1, 0))
        j = jnp.minimum(jnp.where(t < n_left, t, t + n_over), nk - 1)
        off = pl.multiple_of(j * bk, bk)
        side = jnp.where(j < n_left, 0, 1)
        return off, side

    def qk_max_block(slot, t):
        off, side = block_of(t)
        for c in range(2):
            s_ref[slot][c] = _dot_nt(qm_ref[c], k_ref[pl.ds(off, bk), :])
        col_bias(slot, off, side)
        gap = jnp.where(side == 1, off - q0, q0 - off - (bk - 1)).astype(F32)
        max_pass(slot, jnp.where(side == 1, cl, -cl), cl * gap)

    off_o = pl.multiple_of(n_left * bk, bk)
    for c in range(2):
        s_ref[1][c] = _dot_nt(qm_ref[c], k_ref[pl.ds(off_o, bk), :])
    bias_pass(1, off_o)
    col_bias(1, off_o, None)
    max_pass(1, 0.0, 0.0)
    qk_max_block(0, 0)
    exp_pass(1)
    rescale(1)
    pv_block(1, off_o)

    def step(t, slot):
        qk_max_block(1 - slot, t + 1)
        exp_pass(slot)
        rescale(slot)
        pv_block(slot, block_of(t)[0])

    def group(u, carry):
        for n in range(STEPS_PER_TRIP):
            step(STEPS_PER_TRIP * u + n, n % 2)
        return carry

    lax.fori_loop(0, n_other // STEPS_PER_TRIP, group, 0)

    for n in range(STEPS_PER_TRIP - 1):
        @pl.when(n_other % STEPS_PER_TRIP > n)
        def _():
            step(n_other - n_other % STEPS_PER_TRIP + n, n % 2)

    lam = (jnp.exp(jnp.sum(lq1_ref[...] * lk1_ref[...], axis=-1, keepdims=True))
           - jnp.exp(jnp.sum(lq2_ref[...] * lk2_ref[...], axis=-1, keepdims=True)) + lam_init)
    l1 = acc_ref[0, :, DA_V_DIM:DA_V_DIM + 1]
    l2 = acc_ref[1, :, DA_V_DIM:DA_V_DIM + 1]
    o = acc_ref[0, :, 0:DA_V_DIM] / l1 - lam * (acc_ref[1, :, 0:DA_V_DIM] / l2)
    o = o * lax.rsqrt(jnp.mean(o * o, -1, keepdims=True) + SUBLN_EPS) * sg_ref[...] * (1.0 - lam_init)
    o_ref[...] = o.astype(o_ref.dtype)


def diff_attention_core(qkv, lq1, lk1, lq2, lk2, subln_g, B, Lp, seq_len, layer_idx, bq, bk):
    Tp, D3 = qkv.shape
    D = D3 // 3
    H = DA_HEADS
    assert Lp % bk == 0 and bk % bq == 0 and bq % ROWS == 0 and bk % LANES == 0
    nq = Lp // bq
    nk = Lp // bk
    lam_init = 0.8 - 0.6 * math.exp(-0.3 * layer_idx)
    slopes = jnp.asarray([2.0 ** (-(8.0 / H) * (i + 1)) for i in range(H)], F32)
    hd = DA_HEAD_DIM
    grid_spec = pltpu.PrefetchScalarGridSpec(
        num_scalar_prefetch=1,
        grid=(B, H, nq),
        in_specs=[pl.BlockSpec((bq, 2 * hd), lambda b, h, i, s: (b * nq + i, h)),
                  pl.BlockSpec((Lp, 2 * hd), lambda b, h, i, s: (b, H + h),
                               pipeline_mode=pl.Buffered(1)),
                  pl.BlockSpec((Lp, DA_V_DIM), lambda b, h, i, s: (b, 2 * H + h),
                               pipeline_mode=pl.Buffered(1)),
                  pl.BlockSpec((1, hd), lambda b, h, i, s: (0, 0)),
                  pl.BlockSpec((1, hd), lambda b, h, i, s: (0, 0)),
                  pl.BlockSpec((1, hd), lambda b, h, i, s: (0, 0)),
                  pl.BlockSpec((1, hd), lambda b, h, i, s: (0, 0)),
                  pl.BlockSpec((1, DA_V_DIM), lambda b, h, i, s: (0, 0))],
        out_specs=pl.BlockSpec((bq, DA_V_DIM), lambda b, h, i, s: (b * nq + i, h)),
        scratch_shapes=[pltpu.VMEM((Lp, 2 * DA_V_DIM), BF16),
                        pltpu.VMEM((2, bq, 2 * hd), BF16),
                        pltpu.VMEM((2, bq, bk), F32),
                        pltpu.VMEM((2, bq, bk), F32),
                        pltpu.VMEM((2, bq, bk), BF16),
                        pltpu.VMEM((2, bq, bk), BF16),
                        pltpu.VMEM((2, bq, LANES), F32),
                        pltpu.VMEM((2, bq, LANES), F32),
                        pltpu.VMEM((2, bq, LANES), F32),
                        pltpu.VMEM((2, bq, LANES), F32),
                        pltpu.VMEM((2, bq, LANES), F32),
                        pltpu.VMEM((8, bk), F32),
                        pltpu.VMEM((8, bk), F32),
                        pltpu.VMEM((2, bq, 2 * DA_V_DIM), F32)],
    )
    return pl.pallas_call(
        functools.partial(_attn_kernel, bq=bq, bk=bk, nk=nk, seq_len=seq_len, lam_init=lam_init),
        grid_spec=grid_spec,
        out_shape=jax.ShapeDtypeStruct((Tp, D), BF16),
        compiler_params=_cparams("parallel", "parallel", "arbitrary"),
    )(slopes, qkv, qkv, qkv, lq1.reshape(1, hd), lk1.reshape(1, hd), lq2.reshape(1, hd),
      lk2.reshape(1, hd), subln_g.reshape(1, DA_V_DIM))


def _seg_sum(x, ones_bd):
    hi, lo = _split2(x)
    return _dot(hi, ones_bd) + _dot(lo, ones_bd)


def _rw_prep_kernel(x_ref, xp_ref, xn_ref, mu_ref, wrkv_ref, w1_ref, w2_ref, a1_ref, a2_ref,
                    g1_ref, g2_ref, vec_ref, ones_ref,
                    r_ref, v_ref, kk_ref, g_ref, bonus_ref, lw_ref, kd_ref, b_ref,
                    *, bm, Lp, seq_len, lora_w, lora_a):
    i = pl.program_id(0)
    x = x_ref[...]
    row = lax.broadcasted_iota(jnp.int32, (bm, 1), 0)
    pos = (i * bm) % Lp + row
    prev_last = xp_ref[7:8, :]
    next_first = xn_ref[0:1, :]
    x_prev = jnp.where(row == 0, prev_last, pltpu.roll(x, 1, 0))
    x_next = jnp.where(row == bm - 1, next_first, pltpu.roll(x, bm - 1, 0))
    x_prev = jnp.where(pos == 0, 0.0, x_prev)
    x_next = jnp.where(pos == seq_len - 1, 0.0, x_next)
    dxp = x_prev - x
    dxn = x_next - x
    valid = pos < seq_len

    def mix(n):
        return (x + mu_ref[n:n + 1, :] * dxp + mu_ref[6 + n:7 + n, :] * dxn).astype(BF16)

    r = _dot(mix(0), wrkv_ref[0])
    k = _dot(mix(1), wrkv_ref[1])
    v = _dot(mix(2), wrkv_ref[2])

    lane = lax.broadcasted_iota(jnp.int32, (bm, 2 * lora_w), 1)
    th = jnp.tanh(_dot(mix(3), w1_ref[...]))
    lw = [_dot(jnp.where((lane >= n * lora_w) & (lane < (n + 1) * lora_w), th, 0.0).astype(BF16),
               w2_ref[...]) for n in range(2)]
    lane = lax.broadcasted_iota(jnp.int32, (bm, 2 * lora_a), 1)
    ah = _dot(mix(4), a1_ref[...])
    la = [_dot(jnp.where((lane >= n * lora_a) & (lane < (n + 1) * lora_a), ah, 0.0).astype(BF16),
               a2_ref[...]) for n in range(2)]
    g = _dot(jax.nn.sigmoid(_dot(mix(5), g1_ref[...])).astype(BF16), g2_ref[...])

    k_k = vec_ref[4:5, :]
    k_a = vec_ref[5:6, :]
    r_k = vec_ref[6:7, :]
    ones_bd = ones_ref[...]

    kkr = k * k_k
    nrm = jnp.sqrt(_seg_sum(kkr * kkr, ones_bd))
    kk = kkr / jnp.maximum(nrm, 1e-12)
    kk = jnp.where(valid, kk, 0.0)
    vz = jnp.where(valid, v, 0.0)

    kd_sum = jnp.zeros_like(k)
    for n in range(2):
        z = -(vec_ref[n:n + 1, :] + lw[n])
        softplus = jnp.maximum(z, 0.0) + jnp.log(1.0 + jnp.exp(-jnp.abs(z)))
        w_log = -softplus - 0.5
        lw_ref[n] = jnp.where(valid, -jnp.exp(w_log), 0.0)
        a = jax.nn.sigmoid(vec_ref[2 + n:3 + n, :] + la[n])
        kd = k * (1.0 + (a - 1.0) * k_a)
        kd_sum = kd_sum + kd
        kd_ref[n] = jnp.where(valid, kd, 0.0)
        b_ref[n] = kk * a

    bonus = _seg_sum(r * kd_sum * r_k, ones_bd) * v
    r_ref[...] = r
    v_ref[...] = vz
    kk_ref[...] = kk
    g_ref[...] = g
    bonus_ref[...] = bonus


def rwkv_prep(h, mu, w_rkv, w1, w2, a1, a2, g1, g2, w0, a0, k_k, k_a, r_k, Lp, seq_len, bm):
    Tp, D = h.shape
    lora_w = w1.shape[-1]
    lora_a = a1.shape[-1]
    lora_g = g1.shape[-1]
    gpad = -(-lora_g // LANES) * LANES
    mu12 = mu.reshape(12, D)
    w1c = jnp.concatenate([w1[0], w1[1]], axis=1).astype(BF16)
    w2c = jnp.concatenate([w2[0], w2[1]], axis=0).astype(BF16)
    a1c = jnp.concatenate([a1[0], a1[1]], axis=1).astype(BF16)
    a2c = jnp.concatenate([a2[0], a2[1]], axis=0).astype(BF16)
    g1p = jnp.pad(g1, ((0, 0), (0, gpad - lora_g))).astype(BF16)
    g2p = jnp.pad(g2, ((0, gpad - lora_g), (0, 0))).astype(BF16)
    vecs = jnp.concatenate([w0, a0, k_k.reshape(1, D), k_a.reshape(1, D), r_k.reshape(1, D),
                            jnp.zeros((1, D), F32)], axis=0)
    seg = jnp.arange(D) // RW_HEAD_DIM
    ones_bd = (seg[:, None] == seg[None, :]).astype(BF16)
    nb8 = bm // 8
    last8 = Tp // 8 - 1
    const2 = lambda i: (0, 0)
    const3 = lambda i: (0, 0, 0)
    row = pl.BlockSpec((bm, D), lambda i: (i, 0))
    row2 = pl.BlockSpec((2, bm, D), lambda i: (0, i, 0))
    sds = jax.ShapeDtypeStruct((Tp, D), F32)
    sds2 = jax.ShapeDtypeStruct((2, Tp, D), F32)
    return pl.pallas_call(
        functools.partial(_rw_prep_kernel, bm=bm, Lp=Lp, seq_len=seq_len, lora_w=lora_w, lora_a=lora_a),
        grid=(Tp // bm,),
        in_specs=[row,
                  pl.BlockSpec((8, D), lambda i: (jnp.maximum(i * nb8 - 1, 0), 0)),
                  pl.BlockSpec((8, D), lambda i: (jnp.minimum((i + 1) * nb8, last8), 0)),
                  pl.BlockSpec((12, D), const2),
                  pl.BlockSpec((3, D, D), const3),
                  pl.BlockSpec((D, 2 * lora_w), const2),
                  pl.BlockSpec((2 * lora_w, D), const2),
                  pl.BlockSpec((D, 2 * lora_a), const2),
                  pl.BlockSpec((2 * lora_a, D), const2),
                  pl.BlockSpec((D, gpad), const2),
                  pl.BlockSpec((gpad, D), const2),
                  pl.BlockSpec((8, D), const2),
                  pl.BlockSpec((D, D), const2)],
        out_specs=[row, row, row, row, row, row2, row2, row2],
        out_shape=[sds, sds, sds, sds, sds, sds2, sds2, sds2],
        compiler_params=_cparams("parallel"),
    )(h, h, h, mu12, w_rkv.astype(BF16), w1c, w2c, a1c, a2c, g1p, g2p, vecs, ones_bd)


def _rw_scan_kernel(r_ref, v_ref, kk_ref, lw_ref, kd_ref, b_ref, y_ref, s_ref, *, C, D):
    d = pl.program_id(1)
    c = pl.program_id(2)
    G_ = RW_GROUP
    hpg = G_ // C
    fwd = d == 0

    @pl.when(c == 0)
    def _():
        s_ref[...] = jnp.zeros_like(s_ref)

    sgn = jnp.where(fwd, 1, -1)
    ri = lax.broadcasted_iota(jnp.int32, (C, C), 0)
    ci = lax.broadcasted_iota(jnp.int32, (C, C), 1)
    tri = jnp.where((ri - ci) * sgn >= 0, 1.0, 0.0).astype(BF16)

    lw = lw_ref[...]
    hi = lw.astype(BF16)
    r1 = lw - hi.astype(F32)
    mid = r1.astype(BF16)
    lo = (r1 - mid.astype(F32)).astype(BF16)
    G = _dot(tri, hi) + _dot(tri, mid) + _dot(tri, lo)
    g_end = jnp.where(fwd, G[C - 1:C, :], G[0:1, :])
    e_pos = jnp.exp(G)
    e_neg = jnp.exp(-G)
    e_exc = jnp.exp(G - lw)
    e_rem = jnp.exp(g_end - G)
    kk = kk_ref[...]
    b = b_ref[...]
    kd = kd_ref[...]
    At = (-kk * e_exc).astype(BF16)
    Bt = (b * e_neg).astype(BF16)
    Kt = (kd * e_neg).astype(BF16)
    Rt = (r_ref[...] * e_pos).astype(BF16)
    Bh = (b * e_rem).astype(BF16)
    Kh = (kd * e_rem).astype(BF16)
    Vb = v_ref[...].astype(BF16)
    gam = jnp.exp(g_end)

    R = lax.broadcasted_iota(jnp.int32, (G_, G_), 0)
    Cc = lax.broadcasted_iota(jnp.int32, (G_, G_), 1)
    blk = (R // C) == (Cc // RW_HEAD_DIM)
    dt = jnp.where((R // C) == (Cc // C), (R % C - Cc % C) * sgn, -1)
    strict = dt > 0
    incl = dt >= 0
    eye = jnp.where(R == Cc, 1.0, 0.0)
    zero_b = jnp.zeros((G_, G_), BF16)

    def expand(x, q):
        xq = x[:, q * G_:(q + 1) * G_]
        return jnp.where(blk, jnp.concatenate([xq] * hpg, axis=0), zero_b)

    groups = range(D // G_)
    Ae = [expand(At, q) for q in groups]
    Be = [expand(Bt, q) for q in groups]
    Ke = [expand(Kt, q) for q in groups]
    Re = [expand(Rt, q) for q in groups]
    Ve = [expand(Vb, q) for q in groups]
    Sb = [s_ref[q].astype(BF16) for q in groups]
    N = [jnp.where(strict, _dot_nt(Ae[q], Be[q]), 0.0) for q in groups]
    P = [eye + N[q] for q in groups]
    Xb = [N[q].astype(BF16) for q in groups]
    Aak = [jnp.where(strict, _dot_nt(Ae[q], Ke[q]), 0.0).astype(BF16) for q in groups]
    rhs = [(_dot_nt(Ae[q], Sb[q]) + _dot(Aak[q], Ve[q])).astype(BF16) for q in groups]
    p = 1
    while 2 * p < C:
        Xb = [_dot(Xb[q], Xb[q]).astype(BF16) for q in groups]
        P = [P[q] + _dot(P[q].astype(BF16), Xb[q]) for q in groups]
        p *= 2
    Ub = [_dot(P[q].astype(BF16), rhs[q]).astype(BF16) for q in groups]
    Arb = [jnp.where(incl, _dot_nt(Re[q], Be[q]), 0.0).astype(BF16) for q in groups]
    Ark = [jnp.where(incl, _dot_nt(Re[q], Ke[q]), 0.0).astype(BF16) for q in groups]
    for q in groups:
        Y = _dot_nt(Re[q], Sb[q]) + _dot(Arb[q], Ub[q]) + _dot(Ark[q], Ve[q])
        yq = Y[0:C, :]
        for hb in range(1, hpg):
            yq = yq + Y[hb * C:(hb + 1) * C, :]
        y_ref[:, q * G_:(q + 1) * G_] = yq
    for q in groups:
        s_ref[q] = (s_ref[q] * gam[:, q * G_:(q + 1) * G_] + _dot_tn(Ub[q], expand(Bh, q))
                    + _dot_tn(Ve[q], expand(Kh, q)))


def rwkv_scan(r, v, kk, lw, kd, b, B, Lp):
    Tp, D = r.shape
    C = RW_CHUNK
    nch = Lp // C

    def rb(bi, d, c):
        return bi * nch + jnp.where(d == 0, c, nch - 1 - c)

    shared = pl.BlockSpec((C, D), lambda bi, d, c: (rb(bi, d, c), 0))
    perdir = pl.BlockSpec((None, C, D), lambda bi, d, c: (d, rb(bi, d, c), 0))
    return pl.pallas_call(
        functools.partial(_rw_scan_kernel, C=C, D=D),
        grid=(B, 2, nch),
        in_specs=[shared, shared, shared, perdir, perdir, perdir],
        out_specs=perdir,
        out_shape=jax.ShapeDtypeStruct((2, Tp, D), F32),
        scratch_shapes=[pltpu.VMEM((D // RW_GROUP, RW_GROUP, RW_GROUP), F32)],
        compiler_params=_cparams("parallel", "parallel", "arbitrary"),
    )(r, v, kk, lw, kd, b)


def _rw_post_kernel(yf_ref, yb_ref, g_ref, bonus_ref, res_ref, vec_ref, avg_ref, wo_ref,
                    o_ref, ob_ref):
    avg = avg_ref[...]
    y = yf_ref[...] + yb_ref[...]
    ym = _seg_sum(y, avg)
    yc = y - ym
    yv = _seg_sum(yc * yc, avg)
    yn = yc * lax.rsqrt(yv + RW_LNX_EPS) * vec_ref[0:1, :] + vec_ref[1:2, :]
    out = ((yn + bonus_ref[...]) * g_ref[...]).astype(BF16)
    z = DEEPNORM_ALPHA * res_ref[...] + _dot(out, wo_ref[...])
    o = _layer_norm_rows(z, vec_ref[2:3, :], vec_ref[3:4, :])
    o_ref[...] = o
    ob_ref[...] = o.astype(BF16)


def rwkv_post(y2, g, bonus, res, lnx_g, lnx_b, w_o, ln_g, ln_b, bm):
    Tp, D = res.shape
    vecs = jnp.concatenate([lnx_g.reshape(1, D), lnx_b.reshape(1, D), ln_g.reshape(1, D),
                            ln_b.reshape(1, D), jnp.zeros((4, D), F32)], axis=0)
    seg = jnp.arange(D) // RW_HEAD_DIM
    avg = ((seg[:, None] == seg[None, :]).astype(F32) / RW_HEAD_DIM).astype(BF16)
    row = pl.BlockSpec((bm, D), lambda i: (i, 0))
    return pl.pallas_call(
        _rw_post_kernel,
        grid=(Tp // bm,),
        in_specs=[pl.BlockSpec((None, bm, D), lambda i: (0, i, 0)),
                  pl.BlockSpec((None, bm, D), lambda i: (1, i, 0)),
                  row, row, row,
                  pl.BlockSpec((8, D), lambda i: (0, 0)),
                  pl.BlockSpec((D, D), lambda i: (0, 0)),
                  pl.BlockSpec((D, D), lambda i: (0, 0))],
        out_specs=[row, row],
        out_shape=[jax.ShapeDtypeStruct((Tp, D), F32), jax.ShapeDtypeStruct((Tp, D), BF16)],
        compiler_params=_cparams("parallel"),
    )(y2, y2, g, bonus, res, vecs, avg, w_o.astype(BF16))


def _router_kernel(x_ref, wh_ref, wl_ref, b_ref, grow_ref, dcol_ref, drow_ref, cnt_ref):
    x = x_ref[...]
    xh, xl = _split2(x)
    wh = wh_ref[...]
    logits = _dot(xh, wh) + _dot(xh, wl_ref[...]) + _dot(xl, wh) + b_ref[...]
    lane = lax.broadcasted_iota(jnp.int32, logits.shape, 1)
    mx = jnp.max(logits, axis=-1, keepdims=True)
    ex = jnp.exp(logits - mx)
    probs = ex / jnp.sum(ex, axis=-1, keepdims=True)
    big = jnp.int32(LANES)
    m1 = jnp.max(probs, axis=-1, keepdims=True)
    i1 = jnp.min(jnp.where(probs == m1, lane, big), axis=-1, keepdims=True)
    sel1 = lane == i1
    rest = jnp.where(sel1 | (lane >= N_EXPERTS), -1.0, probs)
    m2 = jnp.max(rest, axis=-1, keepdims=True)
    i2 = jnp.min(jnp.where(rest == m2, lane, big), axis=-1, keepdims=True)
    sel2 = lane == i2
    tot = m1 + m2
    gates = jnp.where(sel1, m1 / tot, 0.0) + jnp.where(sel2, m2 / tot, 0.0)
    grow_ref[...] = gates.T

    tm = gates.shape[0]
    sel = jnp.where(gates > 0.0, 1.0, 0.0).astype(BF16)
    ti = lax.broadcasted_iota(jnp.int32, (tm, tm), 0)
    tj = lax.broadcasted_iota(jnp.int32, (tm, tm), 1)
    slot_col = _dot(jnp.where(tj < ti, 1.0, 0.0).astype(BF16), sel)
    dcol_ref[...] = jnp.where(gates > 0.0, slot_col, -1.0)
    excl = _dot_tn(sel, jnp.where(ti < tj, 1.0, 0.0).astype(BF16))
    incl = _dot_tn(sel, jnp.where(ti <= tj, 1.0, 0.0).astype(BF16))
    drow_ref[...] = jnp.where(incl - excl > 0.5, excl, -1.0)
    cnt = jnp.sum(jnp.where(gates > 0.0, 1.0, 0.0), axis=0, keepdims=True)
    cnt_ref[...] = jnp.broadcast_to(cnt, cnt_ref.shape).astype(jnp.int32)


def moe_router(h, w_router, b_router, tm):
    Tp, D = h.shape
    E = w_router.shape[1]
    nt = Tp // tm
    wp = jnp.pad(w_router, ((0, 0), (0, LANES - E)))
    wh = wp.astype(BF16)
    wl = (wp - wh.astype(F32)).astype(BF16)
    bp = jnp.concatenate([b_router, jnp.full((LANES - E,), NEG_BIG, F32)]).reshape(1, LANES)
    return pl.pallas_call(
        _router_kernel,
        grid=(nt,),
        in_specs=[pl.BlockSpec((tm, D), lambda i: (i, 0)),
                  pl.BlockSpec((D, LANES), lambda i: (0, 0)),
                  pl.BlockSpec((D, LANES), lambda i: (0, 0)),
                  pl.BlockSpec((1, LANES), lambda i: (0, 0))],
        out_specs=[pl.BlockSpec((None, LANES, tm), lambda i: (i, 0, 0)),
                   pl.BlockSpec((tm, LANES), lambda i: (i, 0)),
                   pl.BlockSpec((None, LANES, tm), lambda i: (i, 0, 0)),
                   pl.BlockSpec((None, 8, LANES), lambda i: (i, 0, 0))],
        out_shape=[jax.ShapeDtypeStruct((nt, LANES, tm), F32),
                   jax.ShapeDtypeStruct((Tp, LANES), F32),
                   jax.ShapeDtypeStruct((nt, LANES, tm), F32),
                   jax.ShapeDtypeStruct((nt, 8, LANES), jnp.int32)],
        compiler_params=_cparams("parallel"),
    )(h, wh, wl, bp)


def _moe_kernel(cnt_ref, x_ref, grow_ref, dcol_ref, drow_ref, wg_ref, wu_ref, wd_ref,
                y_ref, xg_ref, gc_ref, yo_ref, *, tm, n_f):
    i = pl.program_id(0)
    e = pl.program_id(1)
    f = pl.program_id(2)
    nb = (cnt_ref[i, e] + MOE_ROWS - 1) // MOE_ROWS
    npair = (nb + 1) // 2
    PAIR = 2 * MOE_ROWS

    def rows_of(blk):
        return pl.ds(pl.multiple_of(blk * MOE_ROWS, MOE_ROWS), MOE_ROWS)

    def pair_rows(pr):
        return pl.ds(pl.multiple_of(pr * PAIR, PAIR), PAIR)

    @pl.when((e == 0) & (f == 0))
    def _():
        y_ref[...] = jnp.zeros_like(y_ref)

    @pl.when(f == 0)
    def _():
        x = x_ref[...]
        slot = drow_ref[pl.ds(e, 1), :]
        gate = grow_ref[pl.ds(e, 1), :]

        def gather(pr, carry):
            want = (pr * PAIR + lax.broadcasted_iota(jnp.int32, (PAIR, tm), 0)).astype(F32)
            hit = slot == want
            xg_ref[pair_rows(pr), :] = _dot(jnp.where(hit, 1.0, 0.0).astype(BF16), x).astype(BF16)
            gsel = jnp.sum(jnp.where(hit, gate, 0.0), axis=-1, keepdims=True)
            gc_ref[pair_rows(pr), :] = jnp.broadcast_to(gsel, (PAIR, LANES))
            yo_ref[pair_rows(pr), :] = jnp.zeros((PAIR, yo_ref.shape[1]), F32)
            return carry
        lax.fori_loop(0, npair, gather, 0)

    def ffn(blk, carry):
        xb = xg_ref[rows_of(blk), :]
        g = _dot(xb, wg_ref[...])
        u = _dot(xb, wu_ref[...])
        act = (g * jax.nn.sigmoid(g) * u * gc_ref[rows_of(blk), 0:1]).astype(BF16)
        yo_ref[rows_of(blk), :] += _dot(act, wd_ref[...])
        return carry
    lax.fori_loop(0, nb, ffn, 0)

    @pl.when(f == n_f - 1)
    def _():
        lane_t = lax.broadcasted_iota(jnp.int32, (tm, LANES), 1)
        slot = jnp.sum(jnp.where(lane_t == e, dcol_ref[...], 0.0), axis=-1, keepdims=True)

        def scatter(pr, carry):
            have = (pr * PAIR + lax.broadcasted_iota(jnp.int32, (tm, PAIR), 1)).astype(F32)
            put = jnp.where(slot == have, 1.0, 0.0).astype(BF16)
            yh, yl = _split2(yo_ref[pair_rows(pr), :])
            y_ref[...] += _dot(put, yh) + _dot(put, yl)
            return carry
        lax.fori_loop(0, npair, scatter, 0)


def moe_experts(xb, grow, dcol, drow, cnt, wg, wu, wd, tm, fc):
    Tp, D = xb.shape
    E, _, Fe = wg.shape
    n_f = Fe // fc
    nt = Tp // tm
    grid_spec = pltpu.PrefetchScalarGridSpec(
        num_scalar_prefetch=1,
        grid=(nt, E, n_f),
        in_specs=[pl.BlockSpec((tm, D), lambda i, e, f, c: (i, 0)),
                  pl.BlockSpec((None, LANES, tm), lambda i, e, f, c: (i, 0, 0)),
                  pl.BlockSpec((tm, LANES), lambda i, e, f, c: (i, 0)),
                  pl.BlockSpec((None, LANES, tm), lambda i, e, f, c: (i, 0, 0)),
                  pl.BlockSpec((None, D, fc), lambda i, e, f, c: (e, 0, f)),
                  pl.BlockSpec((None, D, fc), lambda i, e, f, c: (e, 0, f)),
                  pl.BlockSpec((None, fc, D), lambda i, e, f, c: (e, f, 0))],
        out_specs=pl.BlockSpec((tm, D), lambda i, e, f, c: (i, 0)),
        scratch_shapes=[pltpu.VMEM((tm + 2 * MOE_ROWS, D), BF16),
                        pltpu.VMEM((tm + 2 * MOE_ROWS, LANES), F32),
                        pltpu.VMEM((tm + 2 * MOE_ROWS, D), F32)],
    )
    return pl.pallas_call(
        functools.partial(_moe_kernel, tm=tm, n_f=n_f),
        grid_spec=grid_spec,
        out_shape=jax.ShapeDtypeStruct((Tp, D), F32),
        compiler_params=_cparams("parallel", "arbitrary", "arbitrary"),
    )(cnt, xb, grow, dcol, drow, wg, wu, wd)


def _res_ln_kernel(y_ref, res_ref, g_ref, b_ref, o_ref):
    o_ref[...] = _layer_norm_rows(DEEPNORM_ALPHA * res_ref[...] + y_ref[...], g_ref[...], b_ref[...])


def res_ln(y, res, g, b, bm):
    M, N = y.shape
    row = pl.BlockSpec((bm, N), lambda i: (i, 0))
    vec = pl.BlockSpec((1, N), lambda i: (0, 0))
    return pl.pallas_call(
        _res_ln_kernel,
        grid=(M // bm,),
        in_specs=[row, row, vec, vec],
        out_specs=row,
        out_shape=jax.ShapeDtypeStruct((M, N), F32),
        compiler_params=_cparams("parallel"),
    )(y, res, g.reshape(1, N), b.reshape(1, N))


def _largest_divisor(n, cap, mult):
    best = mult
    for c in range(mult, cap + 1, mult):
        if n % c == 0:
            best = c
    return best


def kernel(x, meta, ln_g, ln_b, attn_w_in, attn_w_o, attn_lam_q1, attn_lam_k1, attn_lam_q2, attn_lam_k2, attn_subln_g, ffn_w_gate, ffn_w_up, ffn_w_down, rw_mu, rw_w_rkv, rw_w0, rw_w1, rw_w2, rw_a0, rw_a1, rw_a2, rw_g1, rw_g2, rw_k_k, rw_k_a, rw_r_k, rw_lnx_g, rw_lnx_b, rw_w_o, moe_w_router, moe_b_router, moe_w_gate, moe_w_up, moe_w_down):
    B, S, D = x.shape
    L = S + N_META
    Lp = -(-L // SEQ_MULTIPLE) * SEQ_MULTIPLE
    Tp = B * Lp
    BM = 640
    BM_RW = 256

    h = jnp.concatenate([jnp.broadcast_to(meta[None].astype(x.dtype), (B, N_META, D)), x,
                         jnp.zeros((B, Lp - L, D), x.dtype)], axis=1).reshape(Tp, D)

    qkv = matmul(h, attn_w_in[0].astype(BF16), BF16, BM, 1024)
    o = diff_attention_core(qkv, attn_lam_q1[0], attn_lam_k1[0], attn_lam_q2[0], attn_lam_k2[0],
                            attn_subln_g[0], B, Lp, L, 0, bq=640, bk=1280)
    h, hb = matmul_res_ln(o, attn_w_o[0].astype(BF16), h, ln_g[0, 0], ln_b[0, 0], BM, D)
    F = ffn_w_gate.shape[-1]
    act = ffn_up(hb, ffn_w_gate[0].astype(BF16), ffn_w_up[0].astype(BF16), BM,
                 _largest_divisor(F, 1792, LANES))
    h, hb = matmul_res_ln(act, ffn_w_down[0].astype(BF16), h, ln_g[0, 1], ln_b[0, 1], BM,
                          _largest_divisor(F, 2048, LANES))

    r, v, kk, g, bonus, lw, kd, b = rwkv_prep(
        h, rw_mu[0], rw_w_rkv[0], rw_w1[0], rw_w2[0], rw_a1[0], rw_a2[0], rw_g1[0], rw_g2[0],
        rw_w0[0], rw_a0[0], rw_k_k[0], rw_k_a[0], rw_r_k[0], Lp, L, BM_RW)
    y2 = rwkv_scan(r, v, kk, lw, kd, b, B, Lp)
    h, hb = rwkv_post(y2, g, bonus, h, rw_lnx_g[0], rw_lnx_b[0], rw_w_o[0], ln_g[1, 0], ln_b[1, 0],
                      BM_RW)

    grow, dcol, drow, cnt = moe_router(h, moe_w_router[0], moe_b_router[0], MOE_TILE)
    Fe = moe_w_gate.shape[-1]
    E = moe_w_gate.shape[1]
    y = moe_experts(hb, grow, dcol, drow, cnt[:, 0, :E], moe_w_gate[0].astype(BF16),
                    moe_w_up[0].astype(BF16), moe_w_down[0].astype(BF16), MOE_TILE,
                    _largest_divisor(Fe, 1792, LANES))
    h = res_ln(y, h, ln_g[1, 1], ln_b[1, 1], BM)

    return h.reshape(B, Lp, D)[:, N_META:L]
```

```python
import functools
import math

import jax
import jax.numpy as jnp
from jax import lax
from jax.experimental import pallas as pl
from jax.experimental.pallas import tpu as pltpu

F32 = jnp.float32
BF16 = jnp.bfloat16

N_META = 16
DEPTH = 2
LN_EPS = 1e-5
DEEPNORM_ALPHA = (2.0 * DEPTH) ** 0.25

DA_HEADS = 8
DA_HEAD_DIM = 64
DA_V_DIM = 128
SUBLN_EPS = 1e-5

RW_HEAD_DIM = 64
RW_LNX_EPS = 64e-5
RW_CHUNK = 64
RW_GROUP = 256

N_EXPERTS = 8
LANES = 128

SEQ_MULTIPLE = 1280
VMEM_LIMIT = 56 * 1024 * 1024
NEG_BIG = -1e30
LOG2E = 1.4426950408889634
ROWS = 32
STEPS_PER_TRIP = 4
MOE_TILE = 1280
MOE_ROWS = 128


def _cparams(*sem):
    return pltpu.CompilerParams(dimension_semantics=sem, vmem_limit_bytes=VMEM_LIMIT)


def _dot(a, b):
    return jnp.dot(a, b, preferred_element_type=F32)


def _dot_nt(a, b):
    return lax.dot_general(a, b, (((1,), (1,)), ((), ())), preferred_element_type=F32)


def _dot_tn(a, b):
    return lax.dot_general(a, b, (((0,), (0,)), ((), ())), preferred_element_type=F32)


def _split2(x):
    hi = x.astype(BF16)
    lo = (x - hi.astype(F32)).astype(BF16)
    return hi, lo


def _layer_norm_rows(z, g, b):
    mu = jnp.mean(z, -1, keepdims=True)
    zc = z - mu
    var = jnp.mean(zc * zc, -1, keepdims=True)
    return zc * lax.rsqrt(var + LN_EPS) * g + b


def _mm_kernel(a_ref, w_ref, o_ref):
    o_ref[...] = _dot(a_ref[...].astype(BF16), w_ref[...]).astype(o_ref.dtype)


def matmul(a, w, out_dtype, bm, bn):
    M, K = a.shape
    N = w.shape[1]
    return pl.pallas_call(
        _mm_kernel,
        grid=(N // bn, M // bm),
        in_specs=[pl.BlockSpec((bm, K), lambda j, i: (i, 0)),
                  pl.BlockSpec((K, bn), lambda j, i: (0, j))],
        out_specs=pl.BlockSpec((bm, bn), lambda j, i: (i, j)),
        out_shape=jax.ShapeDtypeStruct((M, N), out_dtype),
        compiler_params=_cparams("parallel", "parallel"),
    )(a, w)


def _mm_res_ln_kernel(a_ref, w_ref, res_ref, g_ref, b_ref, o_ref, ob_ref, acc_ref):
    k = pl.program_id(1)

    @pl.when(k == 0)
    def _():
        acc_ref[...] = jnp.zeros_like(acc_ref)

    acc_ref[...] += _dot(a_ref[...].astype(BF16), w_ref[...])

    @pl.when(k == pl.num_programs(1) - 1)
    def _():
        z = DEEPNORM_ALPHA * res_ref[...] + acc_ref[...]
        o = _layer_norm_rows(z, g_ref[...], b_ref[...])
        o_ref[...] = o
        ob_ref[...] = o.astype(BF16)


def matmul_res_ln(a, w, res, g, b, bm, bk):
    M, K = a.shape
    N = w.shape[1]
    return pl.pallas_call(
        _mm_res_ln_kernel,
        grid=(M // bm, K // bk),
        in_specs=[pl.BlockSpec((bm, bk), lambda i, k: (i, k)),
                  pl.BlockSpec((bk, N), lambda i, k: (k, 0)),
                  pl.BlockSpec((bm, N), lambda i, k: (i, 0)),
                  pl.BlockSpec((1, N), lambda i, k: (0, 0)),
                  pl.BlockSpec((1, N), lambda i, k: (0, 0))],
        out_specs=[pl.BlockSpec((bm, N), lambda i, k: (i, 0)),
                   pl.BlockSpec((bm, N), lambda i, k: (i, 0))],
        out_shape=[jax.ShapeDtypeStruct((M, N), F32), jax.ShapeDtypeStruct((M, N), BF16)],
        scratch_shapes=[pltpu.VMEM((bm, N), F32)],
        compiler_params=_cparams("parallel", "arbitrary"),
    )(a, w, res, g.reshape(1, N), b.reshape(1, N))


def _ffn_up_kernel(x_ref, wg_ref, wu_ref, o_ref):
    x = x_ref[...]
    g = _dot(x, wg_ref[...])
    u = _dot(x, wu_ref[...])
    o_ref[...] = (g * jax.nn.sigmoid(g) * u).astype(o_ref.dtype)


def ffn_up(xb, wg, wu, bm, bn):
    M, D = xb.shape
    F = wg.shape[1]
    return pl.pallas_call(
        _ffn_up_kernel,
        grid=(F // bn, M // bm),
        in_specs=[pl.BlockSpec((bm, D), lambda n, i: (i, 0)),
                  pl.BlockSpec((D, bn), lambda n, i: (0, n)),
                  pl.BlockSpec((D, bn), lambda n, i: (0, n))],
        out_specs=pl.BlockSpec((bm, bn), lambda n, i: (i, n)),
        out_shape=jax.ShapeDtypeStruct((M, F), BF16),
        compiler_params=_cparams("parallel", "parallel"),
    )(xb, wg, wu)


def _attn_kernel(slopes_ref, q_ref, k_ref, v_ref, lq1_ref, lk1_ref, lq2_ref, lk2_ref, sg_ref,
                 o_ref, va_ref, qm_ref, s0_ref, s1_ref, p0_ref, p1_ref, m_ref, al0_ref, al1_ref,
                 sh0_ref, sh1_ref, cb0_ref, cb1_ref, acc_ref,
                 *, bq, bk, nk, seq_len, lam_init):
    hd = DA_HEAD_DIM
    s_ref, p_ref, al_ref = (s0_ref, s1_ref), (p0_ref, p1_ref), (al0_ref, al1_ref)
    sh_ref, cb_ref = (sh0_ref, sh1_ref), (cb0_ref, cb1_ref)
    h = pl.program_id(1)
    qi = pl.program_id(2)
    cl = slopes_ref[h] * LOG2E
    ncc = bk // LANES

    @pl.when(qi == 0)
    def _():
        def build(jb, carry):
            off = pl.multiple_of(jb * bk, bk)
            lane = lax.broadcasted_iota(jnp.int32, (bk, DA_V_DIM), 1)
            va_ref[pl.ds(off, bk), 0:DA_V_DIM] = v_ref[pl.ds(off, bk), :]
            va_ref[pl.ds(off, bk), DA_V_DIM:] = jnp.where(lane == 0, 1.0, 0.0).astype(BF16)
            return carry
        lax.fori_loop(0, nk, build, 0)

    q = q_ref[...].astype(F32) * (hd ** -0.5 * LOG2E)
    lane = lax.broadcasted_iota(jnp.int32, (bq, 2 * hd), 1)
    qm_ref[0] = jnp.where(lane < hd, q, 0.0).astype(BF16)
    qm_ref[1] = jnp.where(lane >= hd, q, 0.0).astype(BF16)

    m_ref[...] = jnp.full(m_ref.shape, NEG_BIG, F32)
    acc_ref[...] = jnp.zeros(acc_ref.shape, F32)
    q0 = qi * bq
    n_left = q0 // bk
    first_right = (q0 + bq + bk - 1) // bk
    n_over = first_right - n_left
    n_other = nk - n_over

    def bias_pass(slot, off):
        for r in range(bq // ROWS):
            rows = slice(r * ROWS, (r + 1) * ROWS)
            qpos = q0 + r * ROWS + lax.broadcasted_iota(jnp.int32, (ROWS, LANES), 0)
            kpos = off + lax.broadcasted_iota(jnp.int32, (ROWS, LANES), 1)
            for cc in range(ncc):
                cols = slice(cc * LANES, (cc + 1) * LANES)
                bias = cl * jnp.abs(qpos - (kpos + cc * LANES)).astype(F32)
                for c in range(2):
                    s_ref[slot][c, rows, cols] = s_ref[slot][c, rows, cols] - bias

    def col_bias(slot, off, side):
        jr = lax.broadcasted_iota(jnp.int32, (8, bk), 1)
        cb = jnp.where(off + jr < seq_len, 0.0, -NEG_BIG)
        if side is not None:
            cb = cb + cl * jnp.where(side == 1, jr, bk - 1 - jr).astype(F32)
        cb_ref[slot][...] = cb

    def max_pass(slot, row_slope, row_const):
        for r in range(bq // ROWS):
            rows = slice(r * ROWS, (r + 1) * ROWS)
            ir = (r * ROWS + lax.broadcasted_iota(jnp.int32, (ROWS, LANES), 0)).astype(F32)
            roff = row_slope * ir - row_const
            for c in range(2):
                mx = s_ref[slot][c, rows, 0:LANES] - cb_ref[slot][0:1, 0:LANES]
                for cc in range(1, ncc):
                    cols = slice(cc * LANES, (cc + 1) * LANES)
                    mx = jnp.maximum(mx, s_ref[slot][c, rows, cols] - cb_ref[slot][0:1, cols])
                mx = jnp.max(mx, axis=-1, keepdims=True)
                m_old = m_ref[c, rows, :]
                m_new = jnp.maximum(m_old, mx + roff)
                m_ref[c, rows, :] = m_new
                al_ref[slot][c, rows, :] = jnp.exp2(m_old - m_new)
                sh_ref[slot][c, rows, :] = m_new - roff

    def exp_pass(slot):
        for r in range(bq // ROWS):
            rows = slice(r * ROWS, (r + 1) * ROWS)
            for c in range(2):
                shift = sh_ref[slot][c, rows, :]
                for cc in range(ncc):
                    cols = slice(cc * LANES, (cc + 1) * LANES)
                    x = s_ref[slot][c, rows, cols] - cb_ref[slot][0:1, cols] - shift
                    p_ref[slot][c, rows, cols] = jnp.exp2(x.astype(BF16))

    def rescale(slot):
        for c in range(2):
            al = al_ref[slot][c]
            acc_ref[c] = jnp.concatenate([al, al], axis=1) * acc_ref[c]

    def pv_block(slot, off):
        vb = va_ref[pl.ds(off, bk), :]
        for c in range(2):
            acc_ref[c] += _dot(p_ref[slot][c], vb)

    def block_of(t):
        t = jnp.clip(t, 0, jnp.maximum(n_other - 1, 0))
        j = jnp.minimum(jnp.where(t < n_left, t, t + n_over), nk - 1)
        off = pl.multiple_of(j * bk, bk)
        side = jnp.where(j < n_left, 0, 1)
        return off, side

    def qk_max_block(slot, t):
        off, side = block_of(t)
        for c in range(2):
            s_ref[slot][c] = _dot_nt(qm_ref[c], k_ref[pl.ds(off, bk), :])
        col_bias(slot, off, side)
        gap = jnp.where(side == 1, off - q0, q0 - off - (bk - 1)).astype(F32)
        max_pass(slot, jnp.where(side == 1, cl, -cl), cl * gap)

    off_o = pl.multiple_of(n_left * bk, bk)
    for c in range(2):
        s_ref[1][c] = _dot_nt(qm_ref[c], k_ref[pl.ds(off_o, bk), :])
    bias_pass(1, off_o)
    col_bias(1, off_o, None)
    max_pass(1, 0.0, 0.0)
    qk_max_block(0, 0)
    exp_pass(1)
    rescale(1)
    pv_block(1, off_o)

    def step(t, slot):
        qk_max_block(1 - slot, t + 1)
        exp_pass(slot)
        rescale(slot)
        pv_block(slot, block_of(t)[0])

    def group(u, carry):
        for n in range(STEPS_PER_TRIP):
            step(STEPS_PER_TRIP * u + n, n % 2)
        return carry

    n_pre = jnp.maximum(n_other - 1, 0)
    lax.fori_loop(0, n_pre // STEPS_PER_TRIP, group, 0)

    for n in range(STEPS_PER_TRIP - 1):
        @pl.when(n_pre % STEPS_PER_TRIP > n)
        def _():
            step(n_pre - n_pre % STEPS_PER_TRIP + n, n % 2)

    for slot in range(2):
        @pl.when((n_other > 0) & (n_pre % 2 == slot))
        def _():
            exp_pass(slot)
            rescale(slot)
            pv_block(slot, block_of(n_pre)[0])

    lam = (jnp.exp(jnp.sum(lq1_ref[...] * lk1_ref[...], axis=-1, keepdims=True))
           - jnp.exp(jnp.sum(lq2_ref[...] * lk2_ref[...], axis=-1, keepdims=True)) + lam_init)
    l1 = acc_ref[0, :, DA_V_DIM:DA_V_DIM + 1]
    l2 = acc_ref[1, :, DA_V_DIM:DA_V_DIM + 1]
    o = acc_ref[0, :, 0:DA_V_DIM] / l1 - lam * (acc_ref[1, :, 0:DA_V_DIM] / l2)
    o = o * lax.rsqrt(jnp.mean(o * o, -1, keepdims=True) + SUBLN_EPS) * sg_ref[...] * (1.0 - lam_init)
    o_ref[...] = o.astype(o_ref.dtype)


def diff_attention_core(qkv, lq1, lk1, lq2, lk2, subln_g, B, Lp, seq_len, layer_idx, bq, bk):
    Tp, D3 = qkv.shape
    D = D3 // 3
    H = DA_HEADS
    assert Lp % bk == 0 and bk % bq == 0 and bq % ROWS == 0 and bk % LANES == 0
    nq = Lp // bq
    nk = Lp // bk
    lam_init = 0.8 - 0.6 * math.exp(-0.3 * layer_idx)
    slopes = jnp.asarray([2.0 ** (-(8.0 / H) * (i + 1)) for i in range(H)], F32)
    hd = DA_HEAD_DIM
    grid_spec = pltpu.PrefetchScalarGridSpec(
        num_scalar_prefetch=1,
        grid=(B, H, nq),
        in_specs=[pl.BlockSpec((bq, 2 * hd), lambda b, h, i, s: (b * nq + i, h)),
                  pl.BlockSpec((Lp, 2 * hd), lambda b, h, i, s: (b, H + h),
                               pipeline_mode=pl.Buffered(1)),
                  pl.BlockSpec((Lp, DA_V_DIM), lambda b, h, i, s: (b, 2 * H + h),
                               pipeline_mode=pl.Buffered(1)),
                  pl.BlockSpec((1, hd), lambda b, h, i, s: (0, 0)),
                  pl.BlockSpec((1, hd), lambda b, h, i, s: (0, 0)),
                  pl.BlockSpec((1, hd), lambda b, h, i, s: (0, 0)),
                  pl.BlockSpec((1, hd), lambda b, h, i, s: (0, 0)),
                  pl.BlockSpec((1, DA_V_DIM), lambda b, h, i, s: (0, 0))],
        out_specs=pl.BlockSpec((bq, DA_V_DIM), lambda b, h, i, s: (b * nq + i, h)),
        scratch_shapes=[pltpu.VMEM((Lp, 2 * DA_V_DIM), BF16),
                        pltpu.VMEM((2, bq, 2 * hd), BF16),
                        pltpu.VMEM((2, bq, bk), F32),
                        pltpu.VMEM((2, bq, bk), F32),
                        pltpu.VMEM((2, bq, bk), BF16),
                        pltpu.VMEM((2, bq, bk), BF16),
                        pltpu.VMEM((2, bq, LANES), F32),
                        pltpu.VMEM((2, bq, LANES), F32),
                        pltpu.VMEM((2, bq, LANES), F32),
                        pltpu.VMEM((2, bq, LANES), F32),
                        pltpu.VMEM((2, bq, LANES), F32),
                        pltpu.VMEM((8, bk), F32),
                        pltpu.VMEM((8, bk), F32),
                        pltpu.VMEM((2, bq, 2 * DA_V_DIM), F32)],
    )
    return pl.pallas_call(
        functools.partial(_attn_kernel, bq=bq, bk=bk, nk=nk, seq_len=seq_len, lam_init=lam_init),
        grid_spec=grid_spec,
        out_shape=jax.ShapeDtypeStruct((Tp, D), BF16),
        compiler_params=_cparams("parallel", "parallel", "arbitrary"),
    )(slopes, qkv, qkv, qkv, lq1.reshape(1, hd), lk1.reshape(1, hd), lq2.reshape(1, hd),
      lk2.reshape(1, hd), subln_g.reshape(1, DA_V_DIM))


def _seg_sum(x, ones_bd):
    hi, lo = _split2(x)
    return _dot(hi, ones_bd) + _dot(lo, ones_bd)


def _rw_prep_kernel(x_ref, xp_ref, xn_ref, mu_ref, wrkv_ref, w1_ref, w2_ref, a1_ref, a2_ref,
                    g1_ref, g2_ref, vec_ref, ones_ref,
                    r_ref, v_ref, kk_ref, g_ref, bonus_ref, lw_ref, kd_ref, b_ref,
                    *, bm, Lp, seq_len, lora_w, lora_a):
    i = pl.program_id(0)
    x = x_ref[...]
    row = lax.broadcasted_iota(jnp.int32, (bm, 1), 0)
    pos = (i * bm) % Lp + row
    prev_last = xp_ref[7:8, :]
    next_first = xn_ref[0:1, :]
    x_prev = jnp.where(row == 0, prev_last, pltpu.roll(x, 1, 0))
    x_next = jnp.where(row == bm - 1, next_first, pltpu.roll(x, bm - 1, 0))
    x_prev = jnp.where(pos == 0, 0.0, x_prev)
    x_next = jnp.where(pos == seq_len - 1, 0.0, x_next)
    dxp = x_prev - x
    dxn = x_next - x
    valid = pos < seq_len

    def mix(n):
        return (x + mu_ref[n:n + 1, :] * dxp + mu_ref[6 + n:7 + n, :] * dxn).astype(BF16)

    r = _dot(mix(0), wrkv_ref[0])
    k = _dot(mix(1), wrkv_ref[1])
    v = _dot(mix(2), wrkv_ref[2])

    lane = lax.broadcasted_iota(jnp.int32, (bm, 2 * lora_w), 1)
    th = jnp.tanh(_dot(mix(3), w1_ref[...]))
    lw = [_dot(jnp.where((lane >= n * lora_w) & (lane < (n + 1) * lora_w), th, 0.0).astype(BF16),
               w2_ref[...]) for n in range(2)]
    lane = lax.broadcasted_iota(jnp.int32, (bm, 2 * lora_a), 1)
    ah = _dot(mix(4), a1_ref[...])
    la = [_dot(jnp.where((lane >= n * lora_a) & (lane < (n + 1) * lora_a), ah, 0.0).astype(BF16),
               a2_ref[...]) for n in range(2)]
    g = _dot(jax.nn.sigmoid(_dot(mix(5), g1_ref[...])).astype(BF16), g2_ref[...])

    k_k = vec_ref[4:5, :]
    k_a = vec_ref[5:6, :]
    r_k = vec_ref[6:7, :]
    ones_bd = ones_ref[...]

    kkr = k * k_k
    nrm = jnp.sqrt(_seg_sum(kkr * kkr, ones_bd))
    kk = kkr / jnp.maximum(nrm, 1e-12)
    kk = jnp.where(valid, kk, 0.0)
    vz = jnp.where(valid, v, 0.0)

    kd_sum = jnp.zeros_like(k)
    for n in range(2):
        z = -(vec_ref[n:n + 1, :] + lw[n])
        softplus = jnp.maximum(z, 0.0) + jnp.log(1.0 + jnp.exp(-jnp.abs(z)))
        w_log = -softplus - 0.5
        lw_ref[n] = jnp.where(valid, -jnp.exp(w_log), 0.0)
        a = jax.nn.sigmoid(vec_ref[2 + n:3 + n, :] + la[n])
        kd = k * (1.0 + (a - 1.0) * k_a)
        kd_sum = kd_sum + kd
        kd_ref[n] = jnp.where(valid, kd, 0.0)
        b_ref[n] = kk * a

    bonus = _seg_sum(r * kd_sum * r_k, ones_bd) * v
    r_ref[...] = r
    v_ref[...] = vz
    kk_ref[...] = kk
    g_ref[...] = g
    bonus_ref[...] = bonus


def rwkv_prep(h, mu, w_rkv, w1, w2, a1, a2, g1, g2, w0, a0, k_k, k_a, r_k, Lp, seq_len, bm):
    Tp, D = h.shape
    lora_w = w1.shape[-1]
    lora_a = a1.shape[-1]
    lora_g = g1.shape[-1]
    gpad = -(-lora_g // LANES) * LANES
    mu12 = mu.reshape(12, D)
    w1c = jnp.concatenate([w1[0], w1[1]], axis=1).astype(BF16)
    w2c = jnp.concatenate([w2[0], w2[1]], axis=0).astype(BF16)
    a1c = jnp.concatenate([a1[0], a1[1]], axis=1).astype(BF16)
    a2c = jnp.concatenate([a2[0], a2[1]], axis=0).astype(BF16)
    g1p = jnp.pad(g1, ((0, 0), (0, gpad - lora_g))).astype(BF16)
    g2p = jnp.pad(g2, ((0, gpad - lora_g), (0, 0))).astype(BF16)
    vecs = jnp.concatenate([w0, a0, k_k.reshape(1, D), k_a.reshape(1, D), r_k.reshape(1, D),
                            jnp.zeros((1, D), F32)], axis=0)
    seg = jnp.arange(D) // RW_HEAD_DIM
    ones_bd = (seg[:, None] == seg[None, :]).astype(BF16)
    nb8 = bm // 8
    last8 = Tp // 8 - 1
    const2 = lambda i: (0, 0)
    const3 = lambda i: (0, 0, 0)
    row = pl.BlockSpec((bm, D), lambda i: (i, 0))
    row2 = pl.BlockSpec((2, bm, D), lambda i: (0, i, 0))
    sds = jax.ShapeDtypeStruct((Tp, D), F32)
    sds2 = jax.ShapeDtypeStruct((2, Tp, D), F32)
    return pl.pallas_call(
        functools.partial(_rw_prep_kernel, bm=bm, Lp=Lp, seq_len=seq_len, lora_w=lora_w, lora_a=lora_a),
        grid=(Tp // bm,),
        in_specs=[row,
                  pl.BlockSpec((8, D), lambda i: (jnp.maximum(i * nb8 - 1, 0), 0)),
                  pl.BlockSpec((8, D), lambda i: (jnp.minimum((i + 1) * nb8, last8), 0)),
                  pl.BlockSpec((12, D), const2),
                  pl.BlockSpec((3, D, D), const3),
                  pl.BlockSpec((D, 2 * lora_w), const2),
                  pl.BlockSpec((2 * lora_w, D), const2),
                  pl.BlockSpec((D, 2 * lora_a), const2),
                  pl.BlockSpec((2 * lora_a, D), const2),
                  pl.BlockSpec((D, gpad), const2),
                  pl.BlockSpec((gpad, D), const2),
                  pl.BlockSpec((8, D), const2),
                  pl.BlockSpec((D, D), const2)],
        out_specs=[row, row, row, row, row, row2, row2, row2],
        out_shape=[sds, sds, sds, sds, sds, sds2, sds2, sds2],
        compiler_params=_cparams("parallel"),
    )(h, h, h, mu12, w_rkv.astype(BF16), w1c, w2c, a1c, a2c, g1p, g2p, vecs, ones_bd)


def _rw_scan_kernel(r_ref, v_ref, kk_ref, lw_ref, kd_ref, b_ref, y_ref, s_ref, *, C, D):
    d = pl.program_id(1)
    c = pl.program_id(2)
    G_ = RW_GROUP
    hpg = G_ // C
    fwd = d == 0

    @pl.when(c == 0)
    def _():
        s_ref[...] = jnp.zeros_like(s_ref)

    sgn = jnp.where(fwd, 1, -1)
    ri = lax.broadcasted_iota(jnp.int32, (C, C), 0)
    ci = lax.broadcasted_iota(jnp.int32, (C, C), 1)
    tri = jnp.where((ri - ci) * sgn >= 0, 1.0, 0.0).astype(BF16)

    lw = lw_ref[...]
    hi = lw.astype(BF16)
    r1 = lw - hi.astype(F32)
    mid = r1.astype(BF16)
    lo = (r1 - mid.astype(F32)).astype(BF16)
    G = _dot(tri, hi) + _dot(tri, mid) + _dot(tri, lo)
    g_end = jnp.where(fwd, G[C - 1:C, :], G[0:1, :])
    e_pos = jnp.exp(G)
    e_neg = jnp.exp(-G)
    e_exc = jnp.exp(G - lw)
    e_rem = jnp.exp(g_end - G)
    kk = kk_ref[...]
    b = b_ref[...]
    kd = kd_ref[...]
    At = (-kk * e_exc).astype(BF16)
    Bt = (b * e_neg).astype(BF16)
    Kt = (kd * e_neg).astype(BF16)
    Rt = (r_ref[...] * e_pos).astype(BF16)
    Bh = (b * e_rem).astype(BF16)
    Kh = (kd * e_rem).astype(BF16)
    Vb = v_ref[...].astype(BF16)
    gam = jnp.exp(g_end)

    R = lax.broadcasted_iota(jnp.int32, (G_, G_), 0)
    Cc = lax.broadcasted_iota(jnp.int32, (G_, G_), 1)
    blk = (R // C) == (Cc // RW_HEAD_DIM)
    dt = jnp.where((R // C) == (Cc // C), (R % C - Cc % C) * sgn, -1)
    strict = dt > 0
    incl = dt >= 0
    eye = jnp.where(R == Cc, 1.0, 0.0)
    zero_b = jnp.zeros((G_, G_), BF16)

    def expand(x, q):
        xq = x[:, q * G_:(q + 1) * G_]
        return jnp.where(blk, jnp.concatenate([xq] * hpg, axis=0), zero_b)

    groups = range(D // G_)
    Ae = [expand(At, q) for q in groups]
    Be = [expand(Bt, q) for q in groups]
    Ke = [expand(Kt, q) for q in groups]
    Re = [expand(Rt, q) for q in groups]
    Ve = [expand(Vb, q) for q in groups]
    Sb = [s_ref[q].astype(BF16) for q in groups]
    N = [jnp.where(strict, _dot_nt(Ae[q], Be[q]), 0.0) for q in groups]
    P = [eye + N[q] for q in groups]
    Xb = [N[q].astype(BF16) for q in groups]
    Aak = [jnp.where(strict, _dot_nt(Ae[q], Ke[q]), 0.0).astype(BF16) for q in groups]
    rhs = [(_dot_nt(Ae[q], Sb[q]) + _dot(Aak[q], Ve[q])).astype(BF16) for q in groups]
    p = 1
    while 2 * p < C:
        Xb = [_dot(Xb[q], Xb[q]).astype(BF16) for q in groups]
        P = [P[q] + _dot(P[q].astype(BF16), Xb[q]) for q in groups]
        p *= 2
    Ub = [_dot(P[q].astype(BF16), rhs[q]).astype(BF16) for q in groups]
    Arb = [jnp.where(incl, _dot_nt(Re[q], Be[q]), 0.0).astype(BF16) for q in groups]
    Ark = [jnp.where(incl, _dot_nt(Re[q], Ke[q]), 0.0).astype(BF16) for q in groups]
    for q in groups:
        Y = _dot_nt(Re[q], Sb[q]) + _dot(Arb[q], Ub[q]) + _dot(Ark[q], Ve[q])
        yq = Y[0:C, :]
        for hb in range(1, hpg):
            yq = yq + Y[hb * C:(hb + 1) * C, :]
        y_ref[:, q * G_:(q + 1) * G_] = yq
    for q in groups:
        s_ref[q] = (s_ref[q] * gam[:, q * G_:(q + 1) * G_] + _dot_tn(Ub[q], expand(Bh, q))
                    + _dot_tn(Ve[q], expand(Kh, q)))


def rwkv_scan(r, v, kk, lw, kd, b, B, Lp):
    Tp, D = r.shape
    C = RW_CHUNK
    nch = Lp // C

    def rb(bi, d, c):
        return bi * nch + jnp.where(d == 0, c, nch - 1 - c)

    shared = pl.BlockSpec((C, D), lambda bi, d, c: (rb(bi, d, c), 0))
    perdir = pl.BlockSpec((None, C, D), lambda bi, d, c: (d, rb(bi, d, c), 0))
    return pl.pallas_call(
        functools.partial(_rw_scan_kernel, C=C, D=D),
        grid=(B, 2, nch),
        in_specs=[shared, shared, shared, perdir, perdir, perdir],
        out_specs=perdir,
        out_shape=jax.ShapeDtypeStruct((2, Tp, D), F32),
        scratch_shapes=[pltpu.VMEM((D // RW_GROUP, RW_GROUP, RW_GROUP), F32)],
        compiler_params=_cparams("parallel", "parallel", "arbitrary"),
    )(r, v, kk, lw, kd, b)


def _rw_post_kernel(yf_ref, yb_ref, g_ref, bonus_ref, res_ref, vec_ref, avg_ref, wo_ref,
                    o_ref, ob_ref):
    avg = avg_ref[...]
    y = yf_ref[...] + yb_ref[...]
    ym = _seg_sum(y, avg)
    yc = y - ym
    yv = _seg_sum(yc * yc, avg)
    yn = yc * lax.rsqrt(yv + RW_LNX_EPS) * vec_ref[0:1, :] + vec_ref[1:2, :]
    out = ((yn + bonus_ref[...]) * g_ref[...]).astype(BF16)
    z = DEEPNORM_ALPHA * res_ref[...] + _dot(out, wo_ref[...])
    o = _layer_norm_rows(z, vec_ref[2:3, :], vec_ref[3:4, :])
    o_ref[...] = o
    ob_ref[...] = o.astype(BF16)


def rwkv_post(y2, g, bonus, res, lnx_g, lnx_b, w_o, ln_g, ln_b, bm):
    Tp, D = res.shape
    vecs = jnp.concatenate([lnx_g.reshape(1, D), lnx_b.reshape(1, D), ln_g.reshape(1, D),
                            ln_b.reshape(1, D), jnp.zeros((4, D), F32)], axis=0)
    seg = jnp.arange(D) // RW_HEAD_DIM
    avg = ((seg[:, None] == seg[None, :]).astype(F32) / RW_HEAD_DIM).astype(BF16)
    row = pl.BlockSpec((bm, D), lambda i: (i, 0))
    return pl.pallas_call(
        _rw_post_kernel,
        grid=(Tp // bm,),
        in_specs=[pl.BlockSpec((None, bm, D), lambda i: (0, i, 0)),
                  pl.BlockSpec((None, bm, D), lambda i: (1, i, 0)),
                  row, row, row,
                  pl.BlockSpec((8, D), lambda i: (0, 0)),
                  pl.BlockSpec((D, D), lambda i: (0, 0)),
                  pl.BlockSpec((D, D), lambda i: (0, 0))],
        out_specs=[row, row],
        out_shape=[jax.ShapeDtypeStruct((Tp, D), F32), jax.ShapeDtypeStruct((Tp, D), BF16)],
        compiler_params=_cparams("parallel"),
    )(y2, y2, g, bonus, res, vecs, avg, w_o.astype(BF16))


def _router_kernel(x_ref, wh_ref, wl_ref, b_ref, grow_ref, dcol_ref, drow_ref, cnt_ref):
    x = x_ref[...]
    xh, xl = _split2(x)
    wh = wh_ref[...]
    logits = _dot(xh, wh) + _dot(xh, wl_ref[...]) + _dot(xl, wh) + b_ref[...]
    lane = lax.broadcasted_iota(jnp.int32, logits.shape, 1)
    mx = jnp.max(logits, axis=-1, keepdims=True)
    ex = jnp.exp(logits - mx)
    probs = ex / jnp.sum(ex, axis=-1, keepdims=True)
    big = jnp.int32(LANES)
    m1 = jnp.max(probs, axis=-1, keepdims=True)
    i1 = jnp.min(jnp.where(probs == m1, lane, big), axis=-1, keepdims=True)
    sel1 = lane == i1
    rest = jnp.where(sel1 | (lane >= N_EXPERTS), -1.0, probs)
    m2 = jnp.max(rest, axis=-1, keepdims=True)
    i2 = jnp.min(jnp.where(rest == m2, lane, big), axis=-1, keepdims=True)
    sel2 = lane == i2
    tot = m1 + m2
    gates = jnp.where(sel1, m1 / tot, 0.0) + jnp.where(sel2, m2 / tot, 0.0)
    grow_ref[...] = gates.T

    tm = gates.shape[0]
    sel = jnp.where(gates > 0.0, 1.0, 0.0).astype(BF16)
    ti = lax.broadcasted_iota(jnp.int32, (tm, tm), 0)
    tj = lax.broadcasted_iota(jnp.int32, (tm, tm), 1)
    slot_col = _dot(jnp.where(tj < ti, 1.0, 0.0).astype(BF16), sel)
    dcol_ref[...] = jnp.where(gates > 0.0, slot_col, -1.0)
    excl = _dot_tn(sel, jnp.where(ti < tj, 1.0, 0.0).astype(BF16))
    incl = _dot_tn(sel, jnp.where(ti <= tj, 1.0, 0.0).astype(BF16))
    drow_ref[...] = jnp.where(incl - excl > 0.5, excl, -1.0)
    cnt = jnp.sum(jnp.where(gates > 0.0, 1.0, 0.0), axis=0, keepdims=True)
    cnt_ref[...] = jnp.broadcast_to(cnt, cnt_ref.shape).astype(jnp.int32)


def moe_router(h, w_router, b_router, tm):
    Tp, D = h.shape
    E = w_router.shape[1]
    nt = Tp // tm
    wp = jnp.pad(w_router, ((0, 0), (0, LANES - E)))
    wh = wp.astype(BF16)
    wl = (wp - wh.astype(F32)).astype(BF16)
    bp = jnp.concatenate([b_router, jnp.full((LANES - E,), NEG_BIG, F32)]).reshape(1, LANES)
    return pl.pallas_call(
        _router_kernel,
        grid=(nt,),
        in_specs=[pl.BlockSpec((tm, D), lambda i: (i, 0)),
                  pl.BlockSpec((D, LANES), lambda i: (0, 0)),
                  pl.BlockSpec((D, LANES), lambda i: (0, 0)),
                  pl.BlockSpec((1, LANES), lambda i: (0, 0))],
        out_specs=[pl.BlockSpec((None, LANES, tm), lambda i: (i, 0, 0)),
                   pl.BlockSpec((tm, LANES), lambda i: (i, 0)),
                   pl.BlockSpec((None, LANES, tm), lambda i: (i, 0, 0)),
                   pl.BlockSpec((None, 8, LANES), lambda i: (i, 0, 0))],
        out_shape=[jax.ShapeDtypeStruct((nt, LANES, tm), F32),
                   jax.ShapeDtypeStruct((Tp, LANES), F32),
                   jax.ShapeDtypeStruct((nt, LANES, tm), F32),
                   jax.ShapeDtypeStruct((nt, 8, LANES), jnp.int32)],
        compiler_params=_cparams("parallel"),
    )(h, wh, wl, bp)


def _moe_kernel(cnt_ref, x_ref, grow_ref, dcol_ref, drow_ref, wg_ref, wu_ref, wd_ref,
                y_ref, xg_ref, gc_ref, yo_ref, *, tm, n_f):
    i = pl.program_id(0)
    e = pl.program_id(1)
    f = pl.program_id(2)
    nb = (cnt_ref[i, e] + MOE_ROWS - 1) // MOE_ROWS
    npair = (nb + 1) // 2
    PAIR = 2 * MOE_ROWS

    def rows_of(blk):
        return pl.ds(pl.multiple_of(blk * MOE_ROWS, MOE_ROWS), MOE_ROWS)

    def pair_rows(pr):
        return pl.ds(pl.multiple_of(pr * PAIR, PAIR), PAIR)

    @pl.when((e == 0) & (f == 0))
    def _():
        y_ref[...] = jnp.zeros_like(y_ref)

    @pl.when(f == 0)
    def _():
        x = x_ref[...]
        slot = drow_ref[pl.ds(e, 1), :]
        gate = grow_ref[pl.ds(e, 1), :]

        def gather(pr, carry):
            want = (pr * PAIR + lax.broadcasted_iota(jnp.int32, (PAIR, tm), 0)).astype(F32)
            hit = slot == want
            xg_ref[pair_rows(pr), :] = _dot(jnp.where(hit, 1.0, 0.0).astype(BF16), x).astype(BF16)
            gsel = jnp.sum(jnp.where(hit, gate, 0.0), axis=-1, keepdims=True)
            gc_ref[pair_rows(pr), :] = jnp.broadcast_to(gsel, (PAIR, LANES))
            yo_ref[pair_rows(pr), :] = jnp.zeros((PAIR, yo_ref.shape[1]), F32)
            return carry
        lax.fori_loop(0, npair, gather, 0)

    def ffn(blk, carry):
        xb = xg_ref[rows_of(blk), :]
        g = _dot(xb, wg_ref[...])
        u = _dot(xb, wu_ref[...])
        act = (g * jax.nn.sigmoid(g) * u * gc_ref[rows_of(blk), 0:1]).astype(BF16)
        yo_ref[rows_of(blk), :] += _dot(act, wd_ref[...])
        return carry
    lax.fori_loop(0, nb, ffn, 0)

    @pl.when(f == n_f - 1)
    def _():
        lane_t = lax.broadcasted_iota(jnp.int32, (tm, LANES), 1)
        slot = jnp.sum(jnp.where(lane_t == e, dcol_ref[...], 0.0), axis=-1, keepdims=True)

        def scatter(pr, carry):
            have = (pr * PAIR + lax.broadcasted_iota(jnp.int32, (tm, PAIR), 1)).astype(F32)
            put = jnp.where(slot == have, 1.0, 0.0).astype(BF16)
            yh, yl = _split2(yo_ref[pair_rows(pr), :])
            y_ref[...] += _dot(put, yh) + _dot(put, yl)
            return carry
        lax.fori_loop(0, npair, scatter, 0)


def moe_experts(xb, grow, dcol, drow, cnt, wg, wu, wd, tm, fc):
    Tp, D = xb.shape
    E, _, Fe = wg.shape
    n_f = Fe // fc
    nt = Tp // tm
    grid_spec = pltpu.PrefetchScalarGridSpec(
        num_scalar_prefetch=1,
        grid=(nt, E, n_f),
        in_specs=[pl.BlockSpec((tm, D), lambda i, e, f, c: (i, 0)),
                  pl.BlockSpec((None, LANES, tm), lambda i, e, f, c: (i, 0, 0)),
                  pl.BlockSpec((tm, LANES), lambda i, e, f, c: (i, 0)),
                  pl.BlockSpec((None, LANES, tm), lambda i, e, f, c: (i, 0, 0)),
                  pl.BlockSpec((None, D, fc), lambda i, e, f, c: (e, 0, f)),
                  pl.BlockSpec((None, D, fc), lambda i, e, f, c: (e, 0, f)),
                  pl.BlockSpec((None, fc, D), lambda i, e, f, c: (e, f, 0))],
        out_specs=pl.BlockSpec((tm, D), lambda i, e, f, c: (i, 0)),
        scratch_shapes=[pltpu.VMEM((tm + 2 * MOE_ROWS, D), BF16),
                        pltpu.VMEM((tm + 2 * MOE_ROWS, LANES), F32),
                        pltpu.VMEM((tm + 2 * MOE_ROWS, D), F32)],
    )
    return pl.pallas_call(
        functools.partial(_moe_kernel, tm=tm, n_f=n_f),
        grid_spec=grid_spec,
        out_shape=jax.ShapeDtypeStruct((Tp, D), F32),
        compiler_params=_cparams("parallel", "arbitrary", "arbitrary"),
    )(cnt, xb, grow, dcol, drow, wg, wu, wd)


def _res_ln_kernel(y_ref, res_ref, g_ref, b_ref, o_ref):
    o_ref[...] = _layer_norm_rows(DEEPNORM_ALPHA * res_ref[...] + y_ref[...], g_ref[...], b_ref[...])


def res_ln(y, res, g, b, bm):
    M, N = y.shape
    row = pl.BlockSpec((bm, N), lambda i: (i, 0))
    vec = pl.BlockSpec((1, N), lambda i: (0, 0))
    return pl.pallas_call(
        _res_ln_kernel,
        grid=(M // bm,),
        in_specs=[row, row, vec, vec],
        out_specs=row,
        out_shape=jax.ShapeDtypeStruct((M, N), F32),
        compiler_params=_cparams("parallel"),
    )(y, res, g.reshape(1, N), b.reshape(1, N))


def _largest_divisor(n, cap, mult):
    best = mult
    for c in range(mult, cap + 1, mult):
        if n % c == 0:
            best = c
    return best


def kernel(x, meta, ln_g, ln_b, attn_w_in, attn_w_o, attn_lam_q1, attn_lam_k1, attn_lam_q2, attn_lam_k2, attn_subln_g, ffn_w_gate, ffn_w_up, ffn_w_down, rw_mu, rw_w_rkv, rw_w0, rw_w1, rw_w2, rw_a0, rw_a1, rw_a2, rw_g1, rw_g2, rw_k_k, rw_k_a, rw_r_k, rw_lnx_g, rw_lnx_b, rw_w_o, moe_w_router, moe_b_router, moe_w_gate, moe_w_up, moe_w_down):
    B, S, D = x.shape
    L = S + N_META
    Lp = -(-L // SEQ_MULTIPLE) * SEQ_MULTIPLE
    Tp = B * Lp
    BM = 640
    BM_RW = 256

    h = jnp.concatenate([jnp.broadcast_to(meta[None].astype(x.dtype), (B, N_META, D)), x,
                         jnp.zeros((B, Lp - L, D), x.dtype)], axis=1).reshape(Tp, D)

    qkv = matmul(h, attn_w_in[0].astype(BF16), BF16, BM, 1024)
    o = diff_attention_core(qkv, attn_lam_q1[0], attn_lam_k1[0], attn_lam_q2[0], attn_lam_k2[0],
                            attn_subln_g[0], B, Lp, L, 0, bq=640, bk=1280)
    h, hb = matmul_res_ln(o, attn_w_o[0].astype(BF16), h, ln_g[0, 0], ln_b[0, 0], BM, D)
    F = ffn_w_gate.shape[-1]
    act = ffn_up(hb, ffn_w_gate[0].astype(BF16), ffn_w_up[0].astype(BF16), BM,
                 _largest_divisor(F, 1792, LANES))
    h, hb = matmul_res_ln(act, ffn_w_down[0].astype(BF16), h, ln_g[0, 1], ln_b[0, 1], BM,
                          _largest_divisor(F, 2048, LANES))

    r, v, kk, g, bonus, lw, kd, b = rwkv_prep(
        h, rw_mu[0], rw_w_rkv[0], rw_w1[0], rw_w2[0], rw_a1[0], rw_a2[0], rw_g1[0], rw_g2[0],
        rw_w0[0], rw_a0[0], rw_k_k[0], rw_k_a[0], rw_r_k[0], Lp, L, BM_RW)
    y2 = rwkv_scan(r, v, kk, lw, kd, b, B, Lp)
    h, hb = rwkv_post(y2, g, bonus, h, rw_lnx_g[0], rw_lnx_b[0], rw_w_o[0], ln_g[1, 0], ln_b[1, 0],
                      BM_RW)

    grow, dcol, drow, cnt = moe_router(h, moe_w_router[0], moe_b_router[0], MOE_TILE)
    Fe = moe_w_gate.shape[-1]
    E = moe_w_gate.shape[1]
    y = moe_experts(hb, grow, dcol, drow, cnt[:, 0, :E], moe_w_gate[0].astype(BF16),
                    moe_w_up[0].astype(BF16), moe_w_down[0].astype(BF16), MOE_TILE,
                    _largest_divisor(Fe, 1792, LANES))
    h = res_ln(y, h, ln_g[1, 1], ln_b[1, 1], BM)

    return h.reshape(B, Lp, D)[:, N_META:L]
```

```python
import functools
import math

import jax
import jax.numpy as jnp
from jax import lax
from jax.experimental import pallas as pl
from jax.experimental.pallas import tpu as pltpu

F32 = jnp.float32
BF16 = jnp.bfloat16

N_META = 16
DEPTH = 2
LN_EPS = 1e-5
DEEPNORM_ALPHA = (2.0 * DEPTH) ** 0.25

DA_HEADS = 8
DA_HEAD_DIM = 64
DA_V_DIM = 128
SUBLN_EPS = 1e-5

RW_HEAD_DIM = 64
RW_LNX_EPS = 64e-5
RW_CHUNK = 64
RW_GROUP = 256

N_EXPERTS = 8
LANES = 128

SEQ_MULTIPLE = 1280
VMEM_LIMIT = 56 * 1024 * 1024
NEG_BIG = -1e30
LOG2E = 1.4426950408889634
ROWS = 32
STEPS_PER_TRIP = 4
MOE_TILE = 1280
MOE_ROWS = 128


def _cparams(*sem):
    return pltpu.CompilerParams(dimension_semantics=sem, vmem_limit_bytes=VMEM_LIMIT)


def _dot(a, b):
    return jnp.dot(a, b, preferred_element_type=F32)


def _dot_nt(a, b):
    return lax.dot_general(a, b, (((1,), (1,)), ((), ())), preferred_element_type=F32)


def _dot_tn(a, b):
    return lax.dot_general(a, b, (((0,), (0,)), ((), ())), preferred_element_type=F32)


def _split2(x):
    hi = x.astype(BF16)
    lo = (x - hi.astype(F32)).astype(BF16)
    return hi, lo


def _layer_norm_rows(z, g, b):
    mu = jnp.mean(z, -1, keepdims=True)
    zc = z - mu
    var = jnp.mean(zc * zc, -1, keepdims=True)
    return zc * lax.rsqrt(var + LN_EPS) * g + b


def _mm_kernel(a_ref, w_ref, o_ref):
    o_ref[...] = _dot(a_ref[...].astype(BF16), w_ref[...]).astype(o_ref.dtype)


def matmul(a, w, out_dtype, bm, bn):
    M, K = a.shape
    N = w.shape[1]
    return pl.pallas_call(
        _mm_kernel,
        grid=(N // bn, M // bm),
        in_specs=[pl.BlockSpec((bm, K), lambda j, i: (i, 0)),
                  pl.BlockSpec((K, bn), lambda j, i: (0, j))],
        out_specs=pl.BlockSpec((bm, bn), lambda j, i: (i, j)),
        out_shape=jax.ShapeDtypeStruct((M, N), out_dtype),
        compiler_params=_cparams("parallel", "parallel"),
    )(a, w)


def _mm_res_ln_kernel(a_ref, w_ref, res_ref, g_ref, b_ref, o_ref, ob_ref, acc_ref):
    k = pl.program_id(1)

    @pl.when(k == 0)
    def _():
        acc_ref[...] = jnp.zeros_like(acc_ref)

    acc_ref[...] += _dot(a_ref[...].astype(BF16), w_ref[...])

    @pl.when(k == pl.num_programs(1) - 1)
    def _():
        z = DEEPNORM_ALPHA * res_ref[...] + acc_ref[...]
        o = _layer_norm_rows(z, g_ref[...], b_ref[...])
        o_ref[...] = o
        ob_ref[...] = o.astype(BF16)


def matmul_res_ln(a, w, res, g, b, bm, bk):
    M, K = a.shape
    N = w.shape[1]
    return pl.pallas_call(
        _mm_res_ln_kernel,
        grid=(M // bm, K // bk),
        in_specs=[pl.BlockSpec((bm, bk), lambda i, k: (i, k)),
                  pl.BlockSpec((bk, N), lambda i, k: (k, 0)),
                  pl.BlockSpec((bm, N), lambda i, k: (i, 0)),
                  pl.BlockSpec((1, N), lambda i, k: (0, 0)),
                  pl.BlockSpec((1, N), lambda i, k: (0, 0))],
        out_specs=[pl.BlockSpec((bm, N), lambda i, k: (i, 0)),
                   pl.BlockSpec((bm, N), lambda i, k: (i, 0))],
        out_shape=[jax.ShapeDtypeStruct((M, N), F32), jax.ShapeDtypeStruct((M, N), BF16)],
        scratch_shapes=[pltpu.VMEM((bm, N), F32)],
        compiler_params=_cparams("parallel", "arbitrary"),
    )(a, w, res, g.reshape(1, N), b.reshape(1, N))


def _ffn_up_kernel(x_ref, wg_ref, wu_ref, o_ref):
    x = x_ref[...]
    g = _dot(x, wg_ref[...])
    u = _dot(x, wu_ref[...])
    o_ref[...] = (g * jax.nn.sigmoid(g) * u).astype(o_ref.dtype)


def ffn_up(xb, wg, wu, bm, bn):
    M, D = xb.shape
    F = wg.shape[1]
    return pl.pallas_call(
        _ffn_up_kernel,
        grid=(F // bn, M // bm),
        in_specs=[pl.BlockSpec((bm, D), lambda n, i: (i, 0)),
                  pl.BlockSpec((D, bn), lambda n, i: (0, n)),
                  pl.BlockSpec((D, bn), lambda n, i: (0, n))],
        out_specs=pl.BlockSpec((bm, bn), lambda n, i: (i, n)),
        out_shape=jax.ShapeDtypeStruct((M, F), BF16),
        compiler_params=_cparams("parallel", "parallel"),
    )(xb, wg, wu)


def _attn_kernel(slopes_ref, q_ref, k_ref, v_ref, lq1_ref, lk1_ref, lq2_ref, lk2_ref, sg_ref,
                 o_ref, va_ref, qm_ref, s0_ref, s1_ref, p0_ref, p1_ref, m_ref, al0_ref, al1_ref,
                 sh0_ref, sh1_ref, cb0_ref, cb1_ref, acc_ref,
                 *, bq, bk, nk, seq_len, lam_init):
    hd = DA_HEAD_DIM
    s_ref, p_ref, al_ref = (s0_ref, s1_ref), (p0_ref, p1_ref), (al0_ref, al1_ref)
    sh_ref, cb_ref = (sh0_ref, sh1_ref), (cb0_ref, cb1_ref)
    h = pl.program_id(1)
    qi = pl.program_id(2)
    cl = slopes_ref[h] * LOG2E
    ncc = bk // LANES

    @pl.when(qi == 0)
    def _():
        def build(jb, carry):
            off = pl.multiple_of(jb * bk, bk)
            lane = lax.broadcasted_iota(jnp.int32, (bk, DA_V_DIM), 1)
            va_ref[pl.ds(off, bk), 0:DA_V_DIM] = v_ref[pl.ds(off, bk), :]
            va_ref[pl.ds(off, bk), DA_V_DIM:] = jnp.where(lane == 0, 1.0, 0.0).astype(BF16)
            return carry
        lax.fori_loop(0, nk, build, 0)

    q = q_ref[...].astype(F32) * (hd ** -0.5 * LOG2E)
    lane = lax.broadcasted_iota(jnp.int32, (bq, 2 * hd), 1)
    qm_ref[0] = jnp.where(lane < hd, q, 0.0).astype(BF16)
    qm_ref[1] = jnp.where(lane >= hd, q, 0.0).astype(BF16)

    m_ref[...] = jnp.full(m_ref.shape, NEG_BIG, F32)
    acc_ref[...] = jnp.zeros(acc_ref.shape, F32)
    q0 = qi * bq
    n_left = q0 // bk
    first_right = (q0 + bq + bk - 1) // bk
    n_over = first_right - n_left
    n_other = nk - n_over

    def bias_pass(slot, off):
        for r in range(bq // ROWS):
            rows = slice(r * ROWS, (r + 1) * ROWS)
            qpos = q0 + r * ROWS + lax.broadcasted_iota(jnp.int32, (ROWS, LANES), 0)
            kpos = off + lax.broadcasted_iota(jnp.int32, (ROWS, LANES), 1)
            for cc in range(ncc):
                cols = slice(cc * LANES, (cc + 1) * LANES)
                bias = cl * jnp.abs(qpos - (kpos + cc * LANES)).astype(F32)
                for c in range(2):
                    s_ref[slot][c, rows, cols] = s_ref[slot][c, rows, cols] - bias

    def col_bias(slot, off, side):
        jr = lax.broadcasted_iota(jnp.int32, (8, bk), 1)
        cb = jnp.where(off + jr < seq_len, 0.0, -NEG_BIG)
        if side is not None:
            cb = cb + cl * jnp.where(side == 1, jr, bk - 1 - jr).astype(F32)
        cb_ref[slot][...] = cb

    def max_pass(slot, row_slope, row_const):
        for r in range(bq // ROWS):
            rows = slice(r * ROWS, (r + 1) * ROWS)
            ir = (r * ROWS + lax.broadcasted_iota(jnp.int32, (ROWS, LANES), 0)).astype(F32)
            roff = row_slope * ir - row_const
            for c in range(2):
                mx = None
                for cc in range(ncc):
                    cols = slice(cc * LANES, (cc + 1) * LANES)
                    d = s_ref[slot][c, rows, cols] - cb_ref[slot][0:1, cols]
                    s_ref[slot][c, rows, cols] = d
                    mx = d if mx is None else jnp.maximum(mx, d)
                mx = jnp.max(mx, axis=-1, keepdims=True)
                m_old = m_ref[c, rows, :]
                m_new = jnp.maximum(m_old, mx + roff)
                m_ref[c, rows, :] = m_new
                al_ref[slot][c, rows, :] = jnp.exp2(m_old - m_new)
                sh_ref[slot][c, rows, :] = m_new - roff

    def exp_pass(slot):
        for r in range(bq // ROWS):
            rows = slice(r * ROWS, (r + 1) * ROWS)
            for c in range(2):
                shift = sh_ref[slot][c, rows, :]
                for cc in range(ncc):
                    cols = slice(cc * LANES, (cc + 1) * LANES)
                    x = s_ref[slot][c, rows, cols] - shift
                    p_ref[slot][c, rows, cols] = jnp.exp2(x.astype(BF16))

    def rescale(slot):
        for c in range(2):
            al = al_ref[slot][c]
            acc_ref[c] = jnp.concatenate([al, al], axis=1) * acc_ref[c]

    def pv_block(slot, off):
        vb = va_ref[pl.ds(off, bk), :]
        for c in range(2):
            acc_ref[c] += _dot(p_ref[slot][c], vb)

    def block_of(t):
        t = jnp.clip(t, 0, jnp.maximum(n_other - 1, 0))
        j = jnp.minimum(jnp.where(t < n_left, t, t + n_over), nk - 1)
        off = pl.multiple_of(j * bk, bk)
        side = jnp.where(j < n_left, 0, 1)
        return off, side

    def qk_max_block(slot, t):
        off, side = block_of(t)
        for c in range(2):
            s_ref[slot][c] = _dot_nt(qm_ref[c], k_ref[pl.ds(off, bk), :])
        col_bias(slot, off, side)
        gap = jnp.where(side == 1, off - q0, q0 - off - (bk - 1)).astype(F32)
        max_pass(slot, jnp.where(side == 1, cl, -cl), cl * gap)

    off_o = pl.multiple_of(n_left * bk, bk)
    for c in range(2):
        s_ref[1][c] = _dot_nt(qm_ref[c], k_ref[pl.ds(off_o, bk), :])
    bias_pass(1, off_o)
    col_bias(1, off_o, None)
    max_pass(1, 0.0, 0.0)
    qk_max_block(0, 0)
    exp_pass(1)
    rescale(1)
    pv_block(1, off_o)

    def step(t, slot):
        qk_max_block(1 - slot, t + 1)
        exp_pass(slot)
        rescale(slot)
        pv_block(slot, block_of(t)[0])

    def group(u, carry):
        for n in range(STEPS_PER_TRIP):
            step(STEPS_PER_TRIP * u + n, n % 2)
        return carry

    n_pre = jnp.maximum(n_other - 1, 0)
    lax.fori_loop(0, n_pre // STEPS_PER_TRIP, group, 0)

    for n in range(STEPS_PER_TRIP - 1):
        @pl.when(n_pre % STEPS_PER_TRIP > n)
        def _():
            step(n_pre - n_pre % STEPS_PER_TRIP + n, n % 2)

    for slot in range(2):
        @pl.when((n_other > 0) & (n_pre % 2 == slot))
        def _():
            exp_pass(slot)
            rescale(slot)
            pv_block(slot, block_of(n_pre)[0])

    lam = (jnp.exp(jnp.sum(lq1_ref[...] * lk1_ref[...], axis=-1, keepdims=True))
           - jnp.exp(jnp.sum(lq2_ref[...] * lk2_ref[...], axis=-1, keepdims=True)) + lam_init)
    l1 = acc_ref[0, :, DA_V_DIM:DA_V_DIM + 1]
    l2 = acc_ref[1, :, DA_V_DIM:DA_V_DIM + 1]
    o = acc_ref[0, :, 0:DA_V_DIM] / l1 - lam * (acc_ref[1, :, 0:DA_V_DIM] / l2)
    o = o * lax.rsqrt(jnp.mean(o * o, -1, keepdims=True) + SUBLN_EPS) * sg_ref[...] * (1.0 - lam_init)
    o_ref[...] = o.astype(o_ref.dtype)


def diff_attention_core(qkv, lq1, lk1, lq2, lk2, subln_g, B, Lp, seq_len, layer_idx, bq, bk):
    Tp, D3 = qkv.shape
    D = D3 // 3
    H = DA_HEADS
    assert Lp % bk == 0 and bk % bq == 0 and bq % ROWS == 0 and bk % LANES == 0
    nq = Lp // bq
    nk = Lp // bk
    lam_init = 0.8 - 0.6 * math.exp(-0.3 * layer_idx)
    slopes = jnp.asarray([2.0 ** (-(8.0 / H) * (i + 1)) for i in range(H)], F32)
    hd = DA_HEAD_DIM
    grid_spec = pltpu.PrefetchScalarGridSpec(
        num_scalar_prefetch=1,
        grid=(B, H, nq),
        in_specs=[pl.BlockSpec((bq, 2 * hd), lambda b, h, i, s: (b * nq + i, h)),
                  pl.BlockSpec((Lp, 2 * hd), lambda b, h, i, s: (b, H + h),
                               pipeline_mode=pl.Buffered(1)),
                  pl.BlockSpec((Lp, DA_V_DIM), lambda b, h, i, s: (b, 2 * H + h),
                               pipeline_mode=pl.Buffered(1)),
                  pl.BlockSpec((1, hd), lambda b, h, i, s: (0, 0)),
                  pl.BlockSpec((1, hd), lambda b, h, i, s: (0, 0)),
                  pl.BlockSpec((1, hd), lambda b, h, i, s: (0, 0)),
                  pl.BlockSpec((1, hd), lambda b, h, i, s: (0, 0)),
                  pl.BlockSpec((1, DA_V_DIM), lambda b, h, i, s: (0, 0))],
        out_specs=pl.BlockSpec((bq, DA_V_DIM), lambda b, h, i, s: (b * nq + i, h)),
        scratch_shapes=[pltpu.VMEM((Lp, 2 * DA_V_DIM), BF16),
                        pltpu.VMEM((2, bq, 2 * hd), BF16),
                        pltpu.VMEM((2, bq, bk), F32),
                        pltpu.VMEM((2, bq, bk), F32),
                        pltpu.VMEM((2, bq, bk), BF16),
                        pltpu.VMEM((2, bq, bk), BF16),
                        pltpu.VMEM((2, bq, LANES), F32),
                        pltpu.VMEM((2, bq, LANES), F32),
                        pltpu.VMEM((2, bq, LANES), F32),
                        pltpu.VMEM((2, bq, LANES), F32),
                        pltpu.VMEM((2, bq, LANES), F32),
                        pltpu.VMEM((8, bk), F32),
                        pltpu.VMEM((8, bk), F32),
                        pltpu.VMEM((2, bq, 2 * DA_V_DIM), F32)],
    )
    return pl.pallas_call(
        functools.partial(_attn_kernel, bq=bq, bk=bk, nk=nk, seq_len=seq_len, lam_init=lam_init),
        grid_spec=grid_spec,
        out_shape=jax.ShapeDtypeStruct((Tp, D), BF16),
        compiler_params=_cparams("parallel", "parallel", "arbitrary"),
    )(slopes, qkv, qkv, qkv, lq1.reshape(1, hd), lk1.reshape(1, hd), lq2.reshape(1, hd),
      lk2.reshape(1, hd), subln_g.reshape(1, DA_V_DIM))


def _seg_sum(x, ones_bd):
    hi, lo = _split2(x)
    return _dot(hi, ones_bd) + _dot(lo, ones_bd)


def _rw_prep_kernel(x_ref, xp_ref, xn_ref, mu_ref, wrkv_ref, w1_ref, w2_ref, a1_ref, a2_ref,
                    g1_ref, g2_ref, vec_ref, ones_ref,
                    r_ref, v_ref, kk_ref, g_ref, bonus_ref, lw_ref, kd_ref, b_ref,
                    *, bm, Lp, seq_len, lora_w, lora_a):
    i = pl.program_id(0)
    x = x_ref[...]
    row = lax.broadcasted_iota(jnp.int32, (bm, 1), 0)
    pos = (i * bm) % Lp + row
    prev_last = xp_ref[7:8, :]
    next_first = xn_ref[0:1, :]
    x_prev = jnp.where(row == 0, prev_last, pltpu.roll(x, 1, 0))
    x_next = jnp.where(row == bm - 1, next_first, pltpu.roll(x, bm - 1, 0))
    x_prev = jnp.where(pos == 0, 0.0, x_prev)
    x_next = jnp.where(pos == seq_len - 1, 0.0, x_next)
    dxp = x_prev - x
    dxn = x_next - x
    valid = pos < seq_len

    def mix(n):
        return (x + mu_ref[n:n + 1, :] * dxp + mu_ref[6 + n:7 + n, :] * dxn).astype(BF16)

    r = _dot(mix(0), wrkv_ref[0])
    k = _dot(mix(1), wrkv_ref[1])
    v = _dot(mix(2), wrkv_ref[2])

    lane = lax.broadcasted_iota(jnp.int32, (bm, 2 * lora_w), 1)
    th = jnp.tanh(_dot(mix(3), w1_ref[...]))
    lw = [_dot(jnp.where((lane >= n * lora_w) & (lane < (n + 1) * lora_w), th, 0.0).astype(BF16),
               w2_ref[...]) for n in range(2)]
    lane = lax.broadcasted_iota(jnp.int32, (bm, 2 * lora_a), 1)
    ah = _dot(mix(4), a1_ref[...])
    la = [_dot(jnp.where((lane >= n * lora_a) & (lane < (n + 1) * lora_a), ah, 0.0).astype(BF16),
               a2_ref[...]) for n in range(2)]
    g = _dot(jax.nn.sigmoid(_dot(mix(5), g1_ref[...])).astype(BF16), g2_ref[...])

    k_k = vec_ref[4:5, :]
    k_a = vec_ref[5:6, :]
    r_k = vec_ref[6:7, :]
    ones_bd = ones_ref[...]

    kkr = k * k_k
    nrm = jnp.sqrt(_seg_sum(kkr * kkr, ones_bd))
    kk = kkr / jnp.maximum(nrm, 1e-12)
    kk = jnp.where(valid, kk, 0.0)
    vz = jnp.where(valid, v, 0.0)

    kd_sum = jnp.zeros_like(k)
    for n in range(2):
        z = -(vec_ref[n:n + 1, :] + lw[n])
        softplus = jnp.maximum(z, 0.0) + jnp.log(1.0 + jnp.exp(-jnp.abs(z)))
        w_log = -softplus - 0.5
        lw_ref[n] = jnp.where(valid, -jnp.exp(w_log), 0.0)
        a = jax.nn.sigmoid(vec_ref[2 + n:3 + n, :] + la[n])
        kd = k * (1.0 + (a - 1.0) * k_a)
        kd_sum = kd_sum + kd
        kd_ref[n] = jnp.where(valid, kd, 0.0)
        b_ref[n] = kk * a

    bonus = _seg_sum(r * kd_sum * r_k, ones_bd) * v
    r_ref[...] = r
    v_ref[...] = vz
    kk_ref[...] = kk
    g_ref[...] = g
    bonus_ref[...] = bonus


def rwkv_prep(h, mu, w_rkv, w1, w2, a1, a2, g1, g2, w0, a0, k_k, k_a, r_k, Lp, seq_len, bm):
    Tp, D = h.shape
    lora_w = w1.shape[-1]
    lora_a = a1.shape[-1]
    lora_g = g1.shape[-1]
    gpad = -(-lora_g // LANES) * LANES
    mu12 = mu.reshape(12, D)
    w1c = jnp.concatenate([w1[0], w1[1]], axis=1).astype(BF16)
    w2c = jnp.concatenate([w2[0], w2[1]], axis=0).astype(BF16)
    a1c = jnp.concatenate([a1[0], a1[1]], axis=1).astype(BF16)
    a2c = jnp.concatenate([a2[0], a2[1]], axis=0).astype(BF16)
    g1p = jnp.pad(g1, ((0, 0), (0, gpad - lora_g))).astype(BF16)
    g2p = jnp.pad(g2, ((0, gpad - lora_g), (0, 0))).astype(BF16)
    vecs = jnp.concatenate([w0, a0, k_k.reshape(1, D), k_a.reshape(1, D), r_k.reshape(1, D),
                            jnp.zeros((1, D), F32)], axis=0)
    seg = jnp.arange(D) // RW_HEAD_DIM
    ones_bd = (seg[:, None] == seg[None, :]).astype(BF16)
    nb8 = bm // 8
    last8 = Tp // 8 - 1
    const2 = lambda i: (0, 0)
    const3 = lambda i: (0, 0, 0)
    row = pl.BlockSpec((bm, D), lambda i: (i, 0))
    row2 = pl.BlockSpec((2, bm, D), lambda i: (0, i, 0))
    sds = jax.ShapeDtypeStruct((Tp, D), F32)
    sds2 = jax.ShapeDtypeStruct((2, Tp, D), F32)
    return pl.pallas_call(
        functools.partial(_rw_prep_kernel, bm=bm, Lp=Lp, seq_len=seq_len, lora_w=lora_w, lora_a=lora_a),
        grid=(Tp // bm,),
        in_specs=[row,
                  pl.BlockSpec((8, D), lambda i: (jnp.maximum(i * nb8 - 1, 0), 0)),
                  pl.BlockSpec((8, D), lambda i: (jnp.minimum((i + 1) * nb8, last8), 0)),
                  pl.BlockSpec((12, D), const2),
                  pl.BlockSpec((3, D, D), const3),
                  pl.BlockSpec((D, 2 * lora_w), const2),
                  pl.BlockSpec((2 * lora_w, D), const2),
                  pl.BlockSpec((D, 2 * lora_a), const2),
                  pl.BlockSpec((2 * lora_a, D), const2),
                  pl.BlockSpec((D, gpad), const2),
                  pl.BlockSpec((gpad, D), const2),
                  pl.BlockSpec((8, D), const2),
                  pl.BlockSpec((D, D), const2)],
        out_specs=[row, row, row, row, row, row2, row2, row2],
        out_shape=[sds, sds, sds, sds, sds, sds2, sds2, sds2],
        compiler_params=_cparams("parallel"),
    )(h, h, h, mu12, w_rkv.astype(BF16), w1c, w2c, a1c, a2c, g1p, g2p, vecs, ones_bd)


def _rw_scan_kernel(r_ref, v_ref, kk_ref, lw_ref, kd_ref, b_ref, y_ref, s_ref, *, C, D):
    d = pl.program_id(1)
    c = pl.program_id(2)
    G_ = RW_GROUP
    hpg = G_ // C
    fwd = d == 0

    @pl.when(c == 0)
    def _():
        s_ref[...] = jnp.zeros_like(s_ref)

    sgn = jnp.where(fwd, 1, -1)
    ri = lax.broadcasted_iota(jnp.int32, (C, C), 0)
    ci = lax.broadcasted_iota(jnp.int32, (C, C), 1)
    tri = jnp.where((ri - ci) * sgn >= 0, 1.0, 0.0).astype(BF16)

    lw = lw_ref[...]
    hi = lw.astype(BF16)
    r1 = lw - hi.astype(F32)
    mid = r1.astype(BF16)
    lo = (r1 - mid.astype(F32)).astype(BF16)
    G = _dot(tri, hi) + _dot(tri, mid) + _dot(tri, lo)
    g_end = jnp.where(fwd, G[C - 1:C, :], G[0:1, :])
    e_pos = jnp.exp(G)
    e_neg = jnp.exp(-G)
    e_exc = jnp.exp(G - lw)
    e_rem = jnp.exp(g_end - G)
    kk = kk_ref[...]
    b = b_ref[...]
    kd = kd_ref[...]
    At = (-kk * e_exc).astype(BF16)
    Bt = (b * e_neg).astype(BF16)
    Kt = (kd * e_neg).astype(BF16)
    Rt = (r_ref[...] * e_pos).astype(BF16)
    Bh = (b * e_rem).astype(BF16)
    Kh = (kd * e_rem).astype(BF16)
    Vb = v_ref[...].astype(BF16)
    gam = jnp.exp(g_end)

    R = lax.broadcasted_iota(jnp.int32, (G_, G_), 0)
    Cc = lax.broadcasted_iota(jnp.int32, (G_, G_), 1)
    blk = (R // C) == (Cc // RW_HEAD_DIM)
    dt = jnp.where((R // C) == (Cc // C), (R % C - Cc % C) * sgn, -1)
    strict = dt > 0
    incl = dt >= 0
    eye = jnp.where(R == Cc, 1.0, 0.0)
    zero_b = jnp.zeros((G_, G_), BF16)

    def expand(x, q):
        xq = x[:, q * G_:(q + 1) * G_]
        return jnp.where(blk, jnp.concatenate([xq] * hpg, axis=0), zero_b)

    groups = range(D // G_)
    Ae = [expand(At, q) for q in groups]
    Be = [expand(Bt, q) for q in groups]
    Ke = [expand(Kt, q) for q in groups]
    Re = [expand(Rt, q) for q in groups]
    Ve = [expand(Vb, q) for q in groups]
    Sb = [s_ref[q].astype(BF16) for q in groups]
    N = [jnp.where(strict, _dot_nt(Ae[q], Be[q]), 0.0) for q in groups]
    P = [eye + N[q] for q in groups]
    Xb = [N[q].astype(BF16) for q in groups]
    Aak = [jnp.where(strict, _dot_nt(Ae[q], Ke[q]), 0.0).astype(BF16) for q in groups]
    rhs = [(_dot_nt(Ae[q], Sb[q]) + _dot(Aak[q], Ve[q])).astype(BF16) for q in groups]
    p = 1
    while 2 * p < C:
        Xb = [_dot(Xb[q], Xb[q]).astype(BF16) for q in groups]
        P = [P[q] + _dot(P[q].astype(BF16), Xb[q]) for q in groups]
        p *= 2
    Ub = [_dot(P[q].astype(BF16), rhs[q]).astype(BF16) for q in groups]
    Arb = [jnp.where(incl, _dot_nt(Re[q], Be[q]), 0.0).astype(BF16) for q in groups]
    Ark = [jnp.where(incl, _dot_nt(Re[q], Ke[q]), 0.0).astype(BF16) for q in groups]
    for q in groups:
        Y = _dot_nt(Re[q], Sb[q]) + _dot(Arb[q], Ub[q]) + _dot(Ark[q], Ve[q])
        yq = Y[0:C, :]
        for hb in range(1, hpg):
            yq = yq + Y[hb * C:(hb + 1) * C, :]
        y_ref[:, q * G_:(q + 1) * G_] = yq
    for q in groups:
        s_ref[q] = (s_ref[q] * gam[:, q * G_:(q + 1) * G_] + _dot_tn(Ub[q], expand(Bh, q))
                    + _dot_tn(Ve[q], expand(Kh, q)))


def rwkv_scan(r, v, kk, lw, kd, b, B, Lp):
    Tp, D = r.shape
    C = RW_CHUNK
    nch = Lp // C

    def rb(bi, d, c):
        return bi * nch + jnp.where(d == 0, c, nch - 1 - c)

    shared = pl.BlockSpec((C, D), lambda bi, d, c: (rb(bi, d, c), 0))
    perdir = pl.BlockSpec((None, C, D), lambda bi, d, c: (d, rb(bi, d, c), 0))
    return pl.pallas_call(
        functools.partial(_rw_scan_kernel, C=C, D=D),
        grid=(B, 2, nch),
        in_specs=[shared, shared, shared, perdir, perdir, perdir],
        out_specs=perdir,
        out_shape=jax.ShapeDtypeStruct((2, Tp, D), F32),
        scratch_shapes=[pltpu.VMEM((D // RW_GROUP, RW_GROUP, RW_GROUP), F32)],
        compiler_params=_cparams("parallel", "parallel", "arbitrary"),
    )(r, v, kk, lw, kd, b)


def _rw_post_kernel(yf_ref, yb_ref, g_ref, bonus_ref, res_ref, vec_ref, avg_ref, wo_ref,
                    o_ref, ob_ref):
    avg = avg_ref[...]
    y = yf_ref[...] + yb_ref[...]
    ym = _seg_sum(y, avg)
    yc = y - ym
    yv = _seg_sum(yc * yc, avg)
    yn = yc * lax.rsqrt(yv + RW_LNX_EPS) * vec_ref[0:1, :] + vec_ref[1:2, :]
    out = ((yn + bonus_ref[...]) * g_ref[...]).astype(BF16)
    z = DEEPNORM_ALPHA * res_ref[...] + _dot(out, wo_ref[...])
    o = _layer_norm_rows(z, vec_ref[2:3, :], vec_ref[3:4, :])
    o_ref[...] = o
    ob_ref[...] = o.astype(BF16)


def rwkv_post(y2, g, bonus, res, lnx_g, lnx_b, w_o, ln_g, ln_b, bm):
    Tp, D = res.shape
    vecs = jnp.concatenate([lnx_g.reshape(1, D), lnx_b.reshape(1, D), ln_g.reshape(1, D),
                            ln_b.reshape(1, D), jnp.zeros((4, D), F32)], axis=0)
    seg = jnp.arange(D) // RW_HEAD_DIM
    avg = ((seg[:, None] == seg[None, :]).astype(F32) / RW_HEAD_DIM).astype(BF16)
    row = pl.BlockSpec((bm, D), lambda i: (i, 0))
    return pl.pallas_call(
        _rw_post_kernel,
        grid=(Tp // bm,),
        in_specs=[pl.BlockSpec((None, bm, D), lambda i: (0, i, 0)),
                  pl.BlockSpec((None, bm, D), lambda i: (1, i, 0)),
                  row, row, row,
                  pl.BlockSpec((8, D), lambda i: (0, 0)),
                  pl.BlockSpec((D, D), lambda i: (0, 0)),
                  pl.BlockSpec((D, D), lambda i: (0, 0))],
        out_specs=[row, row],
        out_shape=[jax.ShapeDtypeStruct((Tp, D), F32), jax.ShapeDtypeStruct((Tp, D), BF16)],
        compiler_params=_cparams("parallel"),
    )(y2, y2, g, bonus, res, vecs, avg, w_o.astype(BF16))


def _router_kernel(x_ref, wh_ref, wl_ref, b_ref, grow_ref, dcol_ref, drow_ref, cnt_ref):
    x = x_ref[...]
    xh, xl = _split2(x)
    wh = wh_ref[...]
    logits = _dot(xh, wh) + _dot(xh, wl_ref[...]) + _dot(xl, wh) + b_ref[...]
    lane = lax.broadcasted_iota(jnp.int32, logits.shape, 1)
    mx = jnp.max(logits, axis=-1, keepdims=True)
    ex = jnp.exp(logits - mx)
    probs = ex / jnp.sum(ex, axis=-1, keepdims=True)
    big = jnp.int32(LANES)
    m1 = jnp.max(probs, axis=-1, keepdims=True)
    i1 = jnp.min(jnp.where(probs == m1, lane, big), axis=-1, keepdims=True)
    sel1 = lane == i1
    rest = jnp.where(sel1 | (lane >= N_EXPERTS), -1.0, probs)
    m2 = jnp.max(rest, axis=-1, keepdims=True)
    i2 = jnp.min(jnp.where(rest == m2, lane, big), axis=-1, keepdims=True)
    sel2 = lane == i2
    tot = m1 + m2
    gates = jnp.where(sel1, m1 / tot, 0.0) + jnp.where(sel2, m2 / tot, 0.0)
    grow_ref[...] = gates.T

    tm = gates.shape[0]
    sel = jnp.where(gates > 0.0, 1.0, 0.0).astype(BF16)
    ti = lax.broadcasted_iota(jnp.int32, (tm, tm), 0)
    tj = lax.broadcasted_iota(jnp.int32, (tm, tm), 1)
    slot_col = _dot(jnp.where(tj < ti, 1.0, 0.0).astype(BF16), sel)
    dcol_ref[...] = jnp.where(gates > 0.0, slot_col, -1.0)
    excl = _dot_tn(sel, jnp.where(ti < tj, 1.0, 0.0).astype(BF16))
    incl = _dot_tn(sel, jnp.where(ti <= tj, 1.0, 0.0).astype(BF16))
    drow_ref[...] = jnp.where(incl - excl > 0.5, excl, -1.0)
    cnt = jnp.sum(jnp.where(gates > 0.0, 1.0, 0.0), axis=0, keepdims=True)
    cnt_ref[...] = jnp.broadcast_to(cnt, cnt_ref.shape).astype(jnp.int32)


def moe_router(h, w_router, b_router, tm):
    Tp, D = h.shape
    E = w_router.shape[1]
    nt = Tp // tm
    wp = jnp.pad(w_router, ((0, 0), (0, LANES - E)))
    wh = wp.astype(BF16)
    wl = (wp - wh.astype(F32)).astype(BF16)
    bp = jnp.concatenate([b_router, jnp.full((LANES - E,), NEG_BIG, F32)]).reshape(1, LANES)
    return pl.pallas_call(
        _router_kernel,
        grid=(nt,),
        in_specs=[pl.BlockSpec((tm, D), lambda i: (i, 0)),
                  pl.BlockSpec((D, LANES), lambda i: (0, 0)),
                  pl.BlockSpec((D, LANES), lambda i: (0, 0)),
                  pl.BlockSpec((1, LANES), lambda i: (0, 0))],
        out_specs=[pl.BlockSpec((None, LANES, tm), lambda i: (i, 0, 0)),
                   pl.BlockSpec((tm, LANES), lambda i: (i, 0)),
                   pl.BlockSpec((None, LANES, tm), lambda i: (i, 0, 0)),
                   pl.BlockSpec((None, 8, LANES), lambda i: (i, 0, 0))],
        out_shape=[jax.ShapeDtypeStruct((nt, LANES, tm), F32),
                   jax.ShapeDtypeStruct((Tp, LANES), F32),
                   jax.ShapeDtypeStruct((nt, LANES, tm), F32),
                   jax.ShapeDtypeStruct((nt, 8, LANES), jnp.int32)],
        compiler_params=_cparams("parallel"),
    )(h, wh, wl, bp)


def _moe_kernel(cnt_ref, x_ref, grow_ref, dcol_ref, drow_ref, wg_ref, wu_ref, wd_ref,
                y_ref, xg_ref, gc_ref, yo_ref, *, tm, n_f):
    i = pl.program_id(0)
    e = pl.program_id(1)
    f = pl.program_id(2)
    nb = (cnt_ref[i, e] + MOE_ROWS - 1) // MOE_ROWS
    npair = (nb + 1) // 2
    PAIR = 2 * MOE_ROWS

    def rows_of(blk):
        return pl.ds(pl.multiple_of(blk * MOE_ROWS, MOE_ROWS), MOE_ROWS)

    def pair_rows(pr):
        return pl.ds(pl.multiple_of(pr * PAIR, PAIR), PAIR)

    @pl.when((e == 0) & (f == 0))
    def _():
        y_ref[...] = jnp.zeros_like(y_ref)

    @pl.when(f == 0)
    def _():
        x = x_ref[...]
        slot = drow_ref[pl.ds(e, 1), :]
        gate = grow_ref[pl.ds(e, 1), :]

        def gather(pr, carry):
            want = (pr * PAIR + lax.broadcasted_iota(jnp.int32, (PAIR, tm), 0)).astype(F32)
            hit = slot == want
            xg_ref[pair_rows(pr), :] = _dot(jnp.where(hit, 1.0, 0.0).astype(BF16), x).astype(BF16)
            gsel = jnp.sum(jnp.where(hit, gate, 0.0), axis=-1, keepdims=True)
            gc_ref[pair_rows(pr), :] = jnp.broadcast_to(gsel, (PAIR, LANES))
            yo_ref[pair_rows(pr), :] = jnp.zeros((PAIR, yo_ref.shape[1]), F32)
            return carry
        lax.fori_loop(0, npair, gather, 0)

    def ffn(blk, carry):
        xb = xg_ref[rows_of(blk), :]
        g = _dot(xb, wg_ref[...])
        u = _dot(xb, wu_ref[...])
        act = (g * jax.nn.sigmoid(g) * u * gc_ref[rows_of(blk), 0:1]).astype(BF16)
        yo_ref[rows_of(blk), :] += _dot(act, wd_ref[...])
        return carry
    lax.fori_loop(0, nb, ffn, 0)

    @pl.when(f == n_f - 1)
    def _():
        lane_t = lax.broadcasted_iota(jnp.int32, (tm, LANES), 1)
        slot = jnp.sum(jnp.where(lane_t == e, dcol_ref[...], 0.0), axis=-1, keepdims=True)

        def scatter(pr, carry):
            have = (pr * PAIR + lax.broadcasted_iota(jnp.int32, (tm, PAIR), 1)).astype(F32)
            put = jnp.where(slot == have, 1.0, 0.0).astype(BF16)
            y_ref[...] += _dot(put, yo_ref[pair_rows(pr), :].astype(BF16))
            return carry
        lax.fori_loop(0, npair, scatter, 0)


def moe_experts(xb, grow, dcol, drow, cnt, wg, wu, wd, tm, fc):
    Tp, D = xb.shape
    E, _, Fe = wg.shape
    n_f = Fe // fc
    nt = Tp // tm
    grid_spec = pltpu.PrefetchScalarGridSpec(
        num_scalar_prefetch=1,
        grid=(nt, E, n_f),
        in_specs=[pl.BlockSpec((tm, D), lambda i, e, f, c: (i, 0)),
                  pl.BlockSpec((None, LANES, tm), lambda i, e, f, c: (i, 0, 0)),
                  pl.BlockSpec((tm, LANES), lambda i, e, f, c: (i, 0)),
                  pl.BlockSpec((None, LANES, tm), lambda i, e, f, c: (i, 0, 0)),
                  pl.BlockSpec((None, D, fc), lambda i, e, f, c: (e, 0, f)),
                  pl.BlockSpec((None, D, fc), lambda i, e, f, c: (e, 0, f)),
                  pl.BlockSpec((None, fc, D), lambda i, e, f, c: (e, f, 0))],
        out_specs=pl.BlockSpec((tm, D), lambda i, e, f, c: (i, 0)),
        scratch_shapes=[pltpu.VMEM((tm + 2 * MOE_ROWS, D), BF16),
                        pltpu.VMEM((tm + 2 * MOE_ROWS, LANES), F32),
                        pltpu.VMEM((tm + 2 * MOE_ROWS, D), F32)],
    )
    return pl.pallas_call(
        functools.partial(_moe_kernel, tm=tm, n_f=n_f),
        grid_spec=grid_spec,
        out_shape=jax.ShapeDtypeStruct((Tp, D), F32),
        compiler_params=_cparams("parallel", "arbitrary", "arbitrary"),
    )(cnt, xb, grow, dcol, drow, wg, wu, wd)


def _res_ln_kernel(y_ref, res_ref, g_ref, b_ref, o_ref):
    o_ref[...] = _layer_norm_rows(DEEPNORM_ALPHA * res_ref[...] + y_ref[...], g_ref[...], b_ref[...])


def res_ln(y, res, g, b, bm):
    M, N = y.shape
    row = pl.BlockSpec((bm, N), lambda i: (i, 0))
    vec = pl.BlockSpec((1, N), lambda i: (0, 0))
    return pl.pallas_call(
        _res_ln_kernel,
        grid=(M // bm,),
        in_specs=[row, row, vec, vec],
        out_specs=row,
        out_shape=jax.ShapeDtypeStruct((M, N), F32),
        compiler_params=_cparams("parallel"),
    )(y, res, g.reshape(1, N), b.reshape(1, N))


def _largest_divisor(n, cap, mult):
    best = mult
    for c in range(mult, cap + 1, mult):
        if n % c == 0:
            best = c
    return best


def kernel(x, meta, ln_g, ln_b, attn_w_in, attn_w_o, attn_lam_q1, attn_lam_k1, attn_lam_q2, attn_lam_k2, attn_subln_g, ffn_w_gate, ffn_w_up, ffn_w_down, rw_mu, rw_w_rkv, rw_w0, rw_w1, rw_w2, rw_a0, rw_a1, rw_a2, rw_g1, rw_g2, rw_k_k, rw_k_a, rw_r_k, rw_lnx_g, rw_lnx_b, rw_w_o, moe_w_router, moe_b_router, moe_w_gate, moe_w_up, moe_w_down):
    B, S, D = x.shape
    L = S + N_META
    Lp = -(-L // SEQ_MULTIPLE) * SEQ_MULTIPLE
    Tp = B * Lp
    BM = 640
    BM_RW = 256

    h = jnp.concatenate([jnp.broadcast_to(meta[None].astype(x.dtype), (B, N_META, D)), x,
                         jnp.zeros((B, Lp - L, D), x.dtype)], axis=1).reshape(Tp, D)

    qkv = matmul(h, attn_w_in[0].astype(BF16), BF16, BM, 1024)
    o = diff_attention_core(qkv, attn_lam_q1[0], attn_lam_k1[0], attn_lam_q2[0], attn_lam_k2[0],
                            attn_subln_g[0], B, Lp, L, 0, bq=640, bk=1280)
    h, hb = matmul_res_ln(o, attn_w_o[0].astype(BF16), h, ln_g[0, 0], ln_b[0, 0], BM, D)
    F = ffn_w_gate.shape[-1]
    act = ffn_up(hb, ffn_w_gate[0].astype(BF16), ffn_w_up[0].astype(BF16), BM,
                 _largest_divisor(F, 1792, LANES))
    h, hb = matmul_res_ln(act, ffn_w_down[0].astype(BF16), h, ln_g[0, 1], ln_b[0, 1], BM,
                          _largest_divisor(F, 2048, LANES))

    r, v, kk, g, bonus, lw, kd, b = rwkv_prep(
        h, rw_mu[0], rw_w_rkv[0], rw_w1[0], rw_w2[0], rw_a1[0], rw_a2[0], rw_g1[0], rw_g2[0],
        rw_w0[0], rw_a0[0], rw_k_k[0], rw_k_a[0], rw_r_k[0], Lp, L, BM_RW)
    y2 = rwkv_scan(r, v, kk, lw, kd, b, B, Lp)
    h, hb = rwkv_post(y2, g, bonus, h, rw_lnx_g[0], rw_lnx_b[0], rw_w_o[0], ln_g[1, 0], ln_b[1, 0],
                      BM_RW)

    grow, dcol, drow, cnt = moe_router(h, moe_w_router[0], moe_b_router[0], MOE_TILE)
    Fe = moe_w_gate.shape[-1]
    E = moe_w_gate.shape[1]
    y = moe_experts(hb, grow, dcol, drow, cnt[:, 0, :E], moe_w_gate[0].astype(BF16),
                    moe_w_up[0].astype(BF16), moe_w_down[0].astype(BF16), MOE_TILE,
                    _largest_divisor(Fe, 1792, LANES))
    h = res_ln(y, h, ln_g[1, 1], ln_b[1, 1], BM)

    return h.reshape(B, Lp, D)[:, N_META:L]
```

```python
import functools
import math

import jax
import jax.numpy as jnp
from jax import lax
from jax.experimental import pallas as pl
from jax.experimental.pallas import tpu as pltpu

F32 = jnp.float32
BF16 = jnp.bfloat16

N_META = 16
DEPTH = 2
LN_EPS = 1e-5
DEEPNORM_ALPHA = (2.0 * DEPTH) ** 0.25

DA_HEADS = 8
DA_HEAD_DIM = 64
DA_V_DIM = 128
SUBLN_EPS = 1e-5

RW_HEAD_DIM = 64
RW_LNX_EPS = 64e-5
RW_CHUNK = 64
RW_GROUP = 256

N_EXPERTS = 8
LANES = 128

SEQ_MULTIPLE = 1280
VMEM_LIMIT = 56 * 1024 * 1024
NEG_BIG = -1e30
LOG2E = 1.4426950408889634
ROWS = 32
STEPS_PER_TRIP = 4
MOE_TILE = 1280
MOE_ROWS = 128


def _cparams(*sem):
    return pltpu.CompilerParams(dimension_semantics=sem, vmem_limit_bytes=VMEM_LIMIT)


def _dot(a, b):
    return jnp.dot(a, b, preferred_element_type=F32)


def _dot_nt(a, b):
    return lax.dot_general(a, b, (((1,), (1,)), ((), ())), preferred_element_type=F32)


def _dot_tn(a, b):
    return lax.dot_general(a, b, (((0,), (0,)), ((), ())), preferred_element_type=F32)


def _split2(x):
    hi = x.astype(BF16)
    lo = (x - hi.astype(F32)).astype(BF16)
    return hi, lo


def _layer_norm_rows(z, g, b):
    mu = jnp.mean(z, -1, keepdims=True)
    zc = z - mu
    var = jnp.mean(zc * zc, -1, keepdims=True)
    return zc * lax.rsqrt(var + LN_EPS) * g + b


def _mm_kernel(a_ref, w_ref, o_ref):
    o_ref[...] = _dot(a_ref[...].astype(BF16), w_ref[...]).astype(o_ref.dtype)


def matmul(a, w, out_dtype, bm, bn):
    M, K = a.shape
    N = w.shape[1]
    return pl.pallas_call(
        _mm_kernel,
        grid=(N // bn, M // bm),
        in_specs=[pl.BlockSpec((bm, K), lambda j, i: (i, 0)),
                  pl.BlockSpec((K, bn), lambda j, i: (0, j))],
        out_specs=pl.BlockSpec((bm, bn), lambda j, i: (i, j)),
        out_shape=jax.ShapeDtypeStruct((M, N), out_dtype),
        compiler_params=_cparams("parallel", "parallel"),
    )(a, w)


def _mm_res_ln_kernel(a_ref, w_ref, res_ref, g_ref, b_ref, o_ref, ob_ref, acc_ref):
    k = pl.program_id(1)

    @pl.when(k == 0)
    def _():
        acc_ref[...] = jnp.zeros_like(acc_ref)

    acc_ref[...] += _dot(a_ref[...].astype(BF16), w_ref[...])

    @pl.when(k == pl.num_programs(1) - 1)
    def _():
        z = DEEPNORM_ALPHA * res_ref[...] + acc_ref[...]
        o = _layer_norm_rows(z, g_ref[...], b_ref[...])
        o_ref[...] = o
        ob_ref[...] = o.astype(BF16)


def matmul_res_ln(a, w, res, g, b, bm, bk):
    M, K = a.shape
    N = w.shape[1]
    return pl.pallas_call(
        _mm_res_ln_kernel,
        grid=(M // bm, K // bk),
        in_specs=[pl.BlockSpec((bm, bk), lambda i, k: (i, k)),
                  pl.BlockSpec((bk, N), lambda i, k: (k, 0)),
                  pl.BlockSpec((bm, N), lambda i, k: (i, 0)),
                  pl.BlockSpec((1, N), lambda i, k: (0, 0)),
                  pl.BlockSpec((1, N), lambda i, k: (0, 0))],
        out_specs=[pl.BlockSpec((bm, N), lambda i, k: (i, 0)),
                   pl.BlockSpec((bm, N), lambda i, k: (i, 0))],
        out_shape=[jax.ShapeDtypeStruct((M, N), F32), jax.ShapeDtypeStruct((M, N), BF16)],
        scratch_shapes=[pltpu.VMEM((bm, N), F32)],
        compiler_params=_cparams("parallel", "arbitrary"),
    )(a, w, res, g.reshape(1, N), b.reshape(1, N))


def _ffn_up_kernel(x_ref, wg_ref, wu_ref, o_ref):
    x = x_ref[...]
    g = _dot(x, wg_ref[...])
    u = _dot(x, wu_ref[...])
    o_ref[...] = (g * jax.nn.sigmoid(g) * u).astype(o_ref.dtype)


def ffn_up(xb, wg, wu, bm, bn):
    M, D = xb.shape
    F = wg.shape[1]
    return pl.pallas_call(
        _ffn_up_kernel,
        grid=(F // bn, M // bm),
        in_specs=[pl.BlockSpec((bm, D), lambda n, i: (i, 0)),
                  pl.BlockSpec((D, bn), lambda n, i: (0, n)),
                  pl.BlockSpec((D, bn), lambda n, i: (0, n))],
        out_specs=pl.BlockSpec((bm, bn), lambda n, i: (i, n)),
        out_shape=jax.ShapeDtypeStruct((M, F), BF16),
        compiler_params=_cparams("parallel", "parallel"),
    )(xb, wg, wu)


def _attn_kernel(slopes_ref, q_ref, k_ref, v_ref, lq1_ref, lk1_ref, lq2_ref, lk2_ref, sg_ref,
                 o_ref, va_ref, qm_ref, s0_ref, s1_ref, p0_ref, p1_ref, m_ref, al0_ref, al1_ref,
                 sh0_ref, sh1_ref, cb0_ref, cb1_ref, acc_ref,
                 *, bq, bk, nk, seq_len, lam_init):
    hd = DA_HEAD_DIM
    s_ref, p_ref, al_ref = (s0_ref, s1_ref), (p0_ref, p1_ref), (al0_ref, al1_ref)
    sh_ref, cb_ref = (sh0_ref, sh1_ref), (cb0_ref, cb1_ref)
    h = pl.program_id(1)
    qi = pl.program_id(2)
    cl = slopes_ref[h] * LOG2E
    ncc = bk // LANES

    @pl.when(qi == 0)
    def _():
        def build(jb, carry):
            off = pl.multiple_of(jb * bk, bk)
            lane = lax.broadcasted_iota(jnp.int32, (bk, DA_V_DIM), 1)
            va_ref[pl.ds(off, bk), 0:DA_V_DIM] = v_ref[pl.ds(off, bk), :]
            va_ref[pl.ds(off, bk), DA_V_DIM:] = jnp.where(lane == 0, 1.0, 0.0).astype(BF16)
            return carry
        lax.fori_loop(0, nk, build, 0)

    q = q_ref[...].astype(F32) * (hd ** -0.5 * LOG2E)
    lane = lax.broadcasted_iota(jnp.int32, (bq, 2 * hd), 1)
    qm_ref[0] = jnp.where(lane < hd, q, 0.0).astype(BF16)
    qm_ref[1] = jnp.where(lane >= hd, q, 0.0).astype(BF16)

    m_ref[...] = jnp.full(m_ref.shape, NEG_BIG, F32)
    acc_ref[...] = jnp.zeros(acc_ref.shape, F32)
    q0 = qi * bq
    n_left = q0 // bk
    first_right = (q0 + bq + bk - 1) // bk
    n_over = first_right - n_left
    n_other = nk - n_over

    def bias_pass(slot, off):
        for r in range(bq // ROWS):
            rows = slice(r * ROWS, (r + 1) * ROWS)
            qpos = q0 + r * ROWS + lax.broadcasted_iota(jnp.int32, (ROWS, LANES), 0)
            kpos = off + lax.broadcasted_iota(jnp.int32, (ROWS, LANES), 1)
            for cc in range(ncc):
                cols = slice(cc * LANES, (cc + 1) * LANES)
                bias = cl * jnp.abs(qpos - (kpos + cc * LANES)).astype(F32)
                for c in range(2):
                    s_ref[slot][c, rows, cols] = s_ref[slot][c, rows, cols] - bias

    def col_bias(slot, off, side):
        jr = lax.broadcasted_iota(jnp.int32, (8, bk), 1)
        cb = jnp.where(off + jr < seq_len, 0.0, -NEG_BIG)
        if side is not None:
            cb = cb + cl * jnp.where(side == 1, jr, bk - 1 - jr).astype(F32)
        cb_ref[slot][...] = cb

    def max_pass(slot, row_slope, row_const):
        for r in range(bq // ROWS):
            rows = slice(r * ROWS, (r + 1) * ROWS)
            ir = (r * ROWS + lax.broadcasted_iota(jnp.int32, (ROWS, LANES), 0)).astype(F32)
            roff = row_slope * ir - row_const
            for c in range(2):
                mx = None
                for cc in range(ncc):
                    cols = slice(cc * LANES, (cc + 1) * LANES)
                    d = s_ref[slot][c, rows, cols] - cb_ref[slot][0:1, cols]
                    s_ref[slot][c, rows, cols] = d
                    mx = d if mx is None else jnp.maximum(mx, d)
                mx = jnp.max(mx, axis=-1, keepdims=True)
                m_old = m_ref[c, rows, :]
                m_new = jnp.maximum(m_old, mx + roff)
                m_ref[c, rows, :] = m_new
                al_ref[slot][c, rows, :] = jnp.exp2(m_old - m_new)
                sh_ref[slot][c, rows, :] = m_new - roff

    def exp_pass(slot):
        for r in range(bq // ROWS):
            rows = slice(r * ROWS, (r + 1) * ROWS)
            for c in range(2):
                shift = sh_ref[slot][c, rows, :]
                for cc in range(ncc):
                    cols = slice(cc * LANES, (cc + 1) * LANES)
                    x = s_ref[slot][c, rows, cols] - shift
                    p_ref[slot][c, rows, cols] = jnp.exp2(x.astype(BF16))

    def pv_block(slot, off):
        vb = va_ref[pl.ds(off, bk), :]
        for c in range(2):
            al = al_ref[slot][c]
            acc_ref[c] = jnp.concatenate([al, al], axis=1) * acc_ref[c] + _dot(p_ref[slot][c], vb)

    def block_of(t):
        t = jnp.clip(t, 0, jnp.maximum(n_other - 1, 0))
        j = jnp.minimum(jnp.where(t < n_left, t, t + n_over), nk - 1)
        off = pl.multiple_of(j * bk, bk)
        side = jnp.where(j < n_left, 0, 1)
        return off, side

    def qk_max_block(slot, t):
        off, side = block_of(t)
        for c in range(2):
            s_ref[slot][c] = _dot_nt(qm_ref[c], k_ref[pl.ds(off, bk), :])
        col_bias(slot, off, side)
        gap = jnp.where(side == 1, off - q0, q0 - off - (bk - 1)).astype(F32)
        max_pass(slot, jnp.where(side == 1, cl, -cl), cl * gap)

    off_o = pl.multiple_of(n_left * bk, bk)
    for c in range(2):
        s_ref[1][c] = _dot_nt(qm_ref[c], k_ref[pl.ds(off_o, bk), :])
    bias_pass(1, off_o)
    col_bias(1, off_o, None)
    max_pass(1, 0.0, 0.0)
    qk_max_block(0, 0)
    exp_pass(1)
    pv_block(1, off_o)

    def step(t, slot):
        qk_max_block(1 - slot, t + 1)
        exp_pass(slot)
        pv_block(slot, block_of(t)[0])

    def group(u, carry):
        for n in range(STEPS_PER_TRIP):
            step(STEPS_PER_TRIP * u + n, n % 2)
        return carry

    n_pre = jnp.maximum(n_other - 1, 0)
    lax.fori_loop(0, n_pre // STEPS_PER_TRIP, group, 0)

    for n in range(STEPS_PER_TRIP - 1):
        @pl.when(n_pre % STEPS_PER_TRIP > n)
        def _():
            step(n_pre - n_pre % STEPS_PER_TRIP + n, n % 2)

    for slot in range(2):
        @pl.when((n_other > 0) & (n_pre % 2 == slot))
        def _():
            exp_pass(slot)
            pv_block(slot, block_of(n_pre)[0])

    lam = (jnp.exp(jnp.sum(lq1_ref[...] * lk1_ref[...], axis=-1, keepdims=True))
           - jnp.exp(jnp.sum(lq2_ref[...] * lk2_ref[...], axis=-1, keepdims=True)) + lam_init)
    l1 = acc_ref[0, :, DA_V_DIM:DA_V_DIM + 1]
    l2 = acc_ref[1, :, DA_V_DIM:DA_V_DIM + 1]
    o = acc_ref[0, :, 0:DA_V_DIM] / l1 - lam * (acc_ref[1, :, 0:DA_V_DIM] / l2)
    o = o * lax.rsqrt(jnp.mean(o * o, -1, keepdims=True) + SUBLN_EPS) * sg_ref[...] * (1.0 - lam_init)
    o_ref[...] = o.astype(o_ref.dtype)


def diff_attention_core(qkv, lq1, lk1, lq2, lk2, subln_g, B, Lp, seq_len, layer_idx, bq, bk):
    Tp, D3 = qkv.shape
    D = D3 // 3
    H = DA_HEADS
    assert Lp % bk == 0 and bk % bq == 0 and bq % ROWS == 0 and bk % LANES == 0
    nq = Lp // bq
    nk = Lp // bk
    lam_init = 0.8 - 0.6 * math.exp(-0.3 * layer_idx)
    slopes = jnp.asarray([2.0 ** (-(8.0 / H) * (i + 1)) for i in range(H)], F32)
    hd = DA_HEAD_DIM
    grid_spec = pltpu.PrefetchScalarGridSpec(
        num_scalar_prefetch=1,
        grid=(B, H, nq),
        in_specs=[pl.BlockSpec((bq, 2 * hd), lambda b, h, i, s: (b * nq + i, h)),
                  pl.BlockSpec((Lp, 2 * hd), lambda b, h, i, s: (b, H + h),
                               pipeline_mode=pl.Buffered(1)),
                  pl.BlockSpec((Lp, DA_V_DIM), lambda b, h, i, s: (b, 2 * H + h),
                               pipeline_mode=pl.Buffered(1)),
                  pl.BlockSpec((1, hd), lambda b, h, i, s: (0, 0)),
                  pl.BlockSpec((1, hd), lambda b, h, i, s: (0, 0)),
                  pl.BlockSpec((1, hd), lambda b, h, i, s: (0, 0)),
                  pl.BlockSpec((1, hd), lambda b, h, i, s: (0, 0)),
                  pl.BlockSpec((1, DA_V_DIM), lambda b, h, i, s: (0, 0))],
        out_specs=pl.BlockSpec((bq, DA_V_DIM), lambda b, h, i, s: (b * nq + i, h)),
        scratch_shapes=[pltpu.VMEM((Lp, 2 * DA_V_DIM), BF16),
                        pltpu.VMEM((2, bq, 2 * hd), BF16),
                        pltpu.VMEM((2, bq, bk), F32),
                        pltpu.VMEM((2, bq, bk), F32),
                        pltpu.VMEM((2, bq, bk), BF16),
                        pltpu.VMEM((2, bq, bk), BF16),
                        pltpu.VMEM((2, bq, LANES), F32),
                        pltpu.VMEM((2, bq, LANES), F32),
                        pltpu.VMEM((2, bq, LANES), F32),
                        pltpu.VMEM((2, bq, LANES), F32),
                        pltpu.VMEM((2, bq, LANES), F32),
                        pltpu.VMEM((8, bk), F32),
                        pltpu.VMEM((8, bk), F32),
                        pltpu.VMEM((2, bq, 2 * DA_V_DIM), F32)],
    )
    return pl.pallas_call(
        functools.partial(_attn_kernel, bq=bq, bk=bk, nk=nk, seq_len=seq_len, lam_init=lam_init),
        grid_spec=grid_spec,
        out_shape=jax.ShapeDtypeStruct((Tp, D), BF16),
        compiler_params=_cparams("parallel", "parallel", "arbitrary"),
    )(slopes, qkv, qkv, qkv, lq1.reshape(1, hd), lk1.reshape(1, hd), lq2.reshape(1, hd),
      lk2.reshape(1, hd), subln_g.reshape(1, DA_V_DIM))


def _seg_sum(x, ones_bd):
    hi, lo = _split2(x)
    return _dot(hi, ones_bd) + _dot(lo, ones_bd)


def _rw_prep_kernel(x_ref, xp_ref, xn_ref, mu_ref, wrkv_ref, w1_ref, w2_ref, a1_ref, a2_ref,
                    g1_ref, g2_ref, vec_ref, ones_ref,
                    r_ref, v_ref, kk_ref, g_ref, bonus_ref, lw_ref, kd_ref, b_ref,
                    *, bm, Lp, seq_len, lora_w, lora_a):
    i = pl.program_id(0)
    x = x_ref[...]
    row = lax.broadcasted_iota(jnp.int32, (bm, 1), 0)
    pos = (i * bm) % Lp + row
    prev_last = xp_ref[7:8, :]
    next_first = xn_ref[0:1, :]
    x_prev = jnp.where(row == 0, prev_last, pltpu.roll(x, 1, 0))
    x_next = jnp.where(row == bm - 1, next_first, pltpu.roll(x, bm - 1, 0))
    x_prev = jnp.where(pos == 0, 0.0, x_prev)
    x_next = jnp.where(pos == seq_len - 1, 0.0, x_next)
    dxp = x_prev - x
    dxn = x_next - x
    valid = pos < seq_len

    def mix(n):
        return (x + mu_ref[n:n + 1, :] * dxp + mu_ref[6 + n:7 + n, :] * dxn).astype(BF16)

    r = _dot(mix(0), wrkv_ref[0])
    k = _dot(mix(1), wrkv_ref[1])
    v = _dot(mix(2), wrkv_ref[2])

    lane = lax.broadcasted_iota(jnp.int32, (bm, 2 * lora_w), 1)
    th = jnp.tanh(_dot(mix(3), w1_ref[...]))
    lw = [_dot(jnp.where((lane >= n * lora_w) & (lane < (n + 1) * lora_w), th, 0.0).astype(BF16),
               w2_ref[...]) for n in range(2)]
    lane = lax.broadcasted_iota(jnp.int32, (bm, 2 * lora_a), 1)
    ah = _dot(mix(4), a1_ref[...])
    la = [_dot(jnp.where((lane >= n * lora_a) & (lane < (n + 1) * lora_a), ah, 0.0).astype(BF16),
               a2_ref[...]) for n in range(2)]
    g = _dot(jax.nn.sigmoid(_dot(mix(5), g1_ref[...])).astype(BF16), g2_ref[...])

    k_k = vec_ref[4:5, :]
    k_a = vec_ref[5:6, :]
    r_k = vec_ref[6:7, :]
    ones_bd = ones_ref[...]

    kkr = k * k_k
    nrm = jnp.sqrt(_seg_sum(kkr * kkr, ones_bd))
    kk = kkr / jnp.maximum(nrm, 1e-12)
    kk = jnp.where(valid, kk, 0.0)
    vz = jnp.where(valid, v, 0.0)

    kd_sum = jnp.zeros_like(k)
    for n in range(2):
        z = -(vec_ref[n:n + 1, :] + lw[n])
        softplus = jnp.maximum(z, 0.0) + jnp.log(1.0 + jnp.exp(-jnp.abs(z)))
        w_log = -softplus - 0.5
        lw_ref[n] = jnp.where(valid, -jnp.exp(w_log), 0.0)
        a = jax.nn.sigmoid(vec_ref[2 + n:3 + n, :] + la[n])
        kd = k * (1.0 + (a - 1.0) * k_a)
        kd_sum = kd_sum + kd
        kd_ref[n] = jnp.where(valid, kd, 0.0)
        b_ref[n] = kk * a

    bonus = _seg_sum(r * kd_sum * r_k, ones_bd) * v
    r_ref[...] = r
    v_ref[...] = vz
    kk_ref[...] = kk
    g_ref[...] = g
    bonus_ref[...] = bonus


def rwkv_prep(h, mu, w_rkv, w1, w2, a1, a2, g1, g2, w0, a0, k_k, k_a, r_k, Lp, seq_len, bm):
    Tp, D = h.shape
    lora_w = w1.shape[-1]
    lora_a = a1.shape[-1]
    lora_g = g1.shape[-1]
    gpad = -(-lora_g // LANES) * LANES
    mu12 = mu.reshape(12, D)
    w1c = jnp.concatenate([w1[0], w1[1]], axis=1).astype(BF16)
    w2c = jnp.concatenate([w2[0], w2[1]], axis=0).astype(BF16)
    a1c = jnp.concatenate([a1[0], a1[1]], axis=1).astype(BF16)
    a2c = jnp.concatenate([a2[0], a2[1]], axis=0).astype(BF16)
    g1p = jnp.pad(g1, ((0, 0), (0, gpad - lora_g))).astype(BF16)
    g2p = jnp.pad(g2, ((0, gpad - lora_g), (0, 0))).astype(BF16)
    vecs = jnp.concatenate([w0, a0, k_k.reshape(1, D), k_a.reshape(1, D), r_k.reshape(1, D),
                            jnp.zeros((1, D), F32)], axis=0)
    seg = jnp.arange(D) // RW_HEAD_DIM
    ones_bd = (seg[:, None] == seg[None, :]).astype(BF16)
    nb8 = bm // 8
    last8 = Tp // 8 - 1
    const2 = lambda i: (0, 0)
    const3 = lambda i: (0, 0, 0)
    row = pl.BlockSpec((bm, D), lambda i: (i, 0))
    row2 = pl.BlockSpec((2, bm, D), lambda i: (0, i, 0))
    sds = jax.ShapeDtypeStruct((Tp, D), F32)
    sds2 = jax.ShapeDtypeStruct((2, Tp, D), F32)
    return pl.pallas_call(
        functools.partial(_rw_prep_kernel, bm=bm, Lp=Lp, seq_len=seq_len, lora_w=lora_w, lora_a=lora_a),
        grid=(Tp // bm,),
        in_specs=[row,
                  pl.BlockSpec((8, D), lambda i: (jnp.maximum(i * nb8 - 1, 0), 0)),
                  pl.BlockSpec((8, D), lambda i: (jnp.minimum((i + 1) * nb8, last8), 0)),
                  pl.BlockSpec((12, D), const2),
                  pl.BlockSpec((3, D, D), const3),
                  pl.BlockSpec((D, 2 * lora_w), const2),
                  pl.BlockSpec((2 * lora_w, D), const2),
                  pl.BlockSpec((D, 2 * lora_a), const2),
                  pl.BlockSpec((2 * lora_a, D), const2),
                  pl.BlockSpec((D, gpad), const2),
                  pl.BlockSpec((gpad, D), const2),
                  pl.BlockSpec((8, D), const2),
                  pl.BlockSpec((D, D), const2)],
        out_specs=[row, row, row, row, row, row2, row2, row2],
        out_shape=[sds, sds, sds, sds, sds, sds2, sds2, sds2],
        compiler_params=_cparams("parallel"),
    )(h, h, h, mu12, w_rkv.astype(BF16), w1c, w2c, a1c, a2c, g1p, g2p, vecs, ones_bd)


def _rw_scan_kernel(r_ref, v_ref, kk_ref, lw_ref, kd_ref, b_ref, y_ref, s_ref, *, C, D):
    d = pl.program_id(1)
    c = pl.program_id(2)
    G_ = RW_GROUP
    hpg = G_ // C
    fwd = d == 0

    @pl.when(c == 0)
    def _():
        s_ref[...] = jnp.zeros_like(s_ref)

    sgn = jnp.where(fwd, 1, -1)
    ri = lax.broadcasted_iota(jnp.int32, (C, C), 0)
    ci = lax.broadcasted_iota(jnp.int32, (C, C), 1)
    tri = jnp.where((ri - ci) * sgn >= 0, 1.0, 0.0).astype(BF16)

    lw = lw_ref[...]
    hi = lw.astype(BF16)
    r1 = lw - hi.astype(F32)
    mid = r1.astype(BF16)
    lo = (r1 - mid.astype(F32)).astype(BF16)
    G = _dot(tri, hi) + _dot(tri, mid) + _dot(tri, lo)
    g_end = jnp.where(fwd, G[C - 1:C, :], G[0:1, :])
    e_pos = jnp.exp(G)
    e_neg = jnp.exp(-G)
    e_exc = jnp.exp(G - lw)
    e_rem = jnp.exp(g_end - G)
    kk = kk_ref[...]
    b = b_ref[...]
    kd = kd_ref[...]
    At = (-kk * e_exc).astype(BF16)
    Bt = (b * e_neg).astype(BF16)
    Kt = (kd * e_neg).astype(BF16)
    Rt = (r_ref[...] * e_pos).astype(BF16)
    Bh = (b * e_rem).astype(BF16)
    Kh = (kd * e_rem).astype(BF16)
    Vb = v_ref[...].astype(BF16)
    gam = jnp.exp(g_end)

    R = lax.broadcasted_iota(jnp.int32, (G_, G_), 0)
    Cc = lax.broadcasted_iota(jnp.int32, (G_, G_), 1)
    blk = (R // C) == (Cc // RW_HEAD_DIM)
    dt = jnp.where((R // C) == (Cc // C), (R % C - Cc % C) * sgn, -1)
    strict = dt > 0
    incl = dt >= 0
    eye = jnp.where(R == Cc, 1.0, 0.0)
    zero_b = jnp.zeros((G_, G_), BF16)

    def expand(x, q):
        xq = x[:, q * G_:(q + 1) * G_]
        return jnp.where(blk, jnp.concatenate([xq] * hpg, axis=0), zero_b)

    groups = range(D // G_)
    Ae = [expand(At, q) for q in groups]
    Be = [expand(Bt, q) for q in groups]
    Ke = [expand(Kt, q) for q in groups]
    Re = [expand(Rt, q) for q in groups]
    Ve = [expand(Vb, q) for q in groups]
    Sb = [s_ref[q].astype(BF16) for q in groups]
    N = [jnp.where(strict, _dot_nt(Ae[q], Be[q]), 0.0) for q in groups]
    P = [eye + N[q] for q in groups]
    Xb = [N[q].astype(BF16) for q in groups]
    Aak = [jnp.where(strict, _dot_nt(Ae[q], Ke[q]), 0.0).astype(BF16) for q in groups]
    rhs = [(_dot_nt(Ae[q], Sb[q]) + _dot(Aak[q], Ve[q])).astype(BF16) for q in groups]
    p = 1
    while 2 * p < C:
        Xb = [_dot(Xb[q], Xb[q]).astype(BF16) for q in groups]
        P = [P[q] + _dot(P[q].astype(BF16), Xb[q]) for q in groups]
        p *= 2
    Ub = [_dot(P[q].astype(BF16), rhs[q]).astype(BF16) for q in groups]
    Arb = [jnp.where(incl, _dot_nt(Re[q], Be[q]), 0.0).astype(BF16) for q in groups]
    Ark = [jnp.where(incl, _dot_nt(Re[q], Ke[q]), 0.0).astype(BF16) for q in groups]
    for q in groups:
        Y = _dot_nt(Re[q], Sb[q]) + _dot(Arb[q], Ub[q]) + _dot(Ark[q], Ve[q])
        yq = Y[0:C, :]
        for hb in range(1, hpg):
            yq = yq + Y[hb * C:(hb + 1) * C, :]
        y_ref[:, q * G_:(q + 1) * G_] = yq
    for q in groups:
        s_ref[q] = (s_ref[q] * gam[:, q * G_:(q + 1) * G_] + _dot_tn(Ub[q], expand(Bh, q))
                    + _dot_tn(Ve[q], expand(Kh, q)))


def rwkv_scan(r, v, kk, lw, kd, b, B, Lp):
    Tp, D = r.shape
    C = RW_CHUNK
    nch = Lp // C

    def rb(bi, d, c):
        return bi * nch + jnp.where(d == 0, c, nch - 1 - c)

    shared = pl.BlockSpec((C, D), lambda bi, d, c: (rb(bi, d, c), 0))
    perdir = pl.BlockSpec((None, C, D), lambda bi, d, c: (d, rb(bi, d, c), 0))
    return pl.pallas_call(
        functools.partial(_rw_scan_kernel, C=C, D=D),
        grid=(B, 2, nch),
        in_specs=[shared, shared, shared, perdir, perdir, perdir],
        out_specs=perdir,
        out_shape=jax.ShapeDtypeStruct((2, Tp, D), F32),
        scratch_shapes=[pltpu.VMEM((D // RW_GROUP, RW_GROUP, RW_GROUP), F32)],
        compiler_params=_cparams("parallel", "parallel", "arbitrary"),
    )(r, v, kk, lw, kd, b)


def _rw_post_kernel(yf_ref, yb_ref, g_ref, bonus_ref, res_ref, vec_ref, avg_ref, wo_ref,
                    o_ref, ob_ref):
    avg = avg_ref[...]
    y = yf_ref[...] + yb_ref[...]
    ym = _seg_sum(y, avg)
    yc = y - ym
    yv = _seg_sum(yc * yc, avg)
    yn = yc * lax.rsqrt(yv + RW_LNX_EPS) * vec_ref[0:1, :] + vec_ref[1:2, :]
    out = ((yn + bonus_ref[...]) * g_ref[...]).astype(BF16)
    z = DEEPNORM_ALPHA * res_ref[...] + _dot(out, wo_ref[...])
    o = _layer_norm_rows(z, vec_ref[2:3, :], vec_ref[3:4, :])
    o_ref[...] = o
    ob_ref[...] = o.astype(BF16)


def rwkv_post(y2, g, bonus, res, lnx_g, lnx_b, w_o, ln_g, ln_b, bm):
    Tp, D = res.shape
    vecs = jnp.concatenate([lnx_g.reshape(1, D), lnx_b.reshape(1, D), ln_g.reshape(1, D),
                            ln_b.reshape(1, D), jnp.zeros((4, D), F32)], axis=0)
    seg = jnp.arange(D) // RW_HEAD_DIM
    avg = ((seg[:, None] == seg[None, :]).astype(F32) / RW_HEAD_DIM).astype(BF16)
    row = pl.BlockSpec((bm, D), lambda i: (i, 0))
    return pl.pallas_call(
        _rw_post_kernel,
        grid=(Tp // bm,),
        in_specs=[pl.BlockSpec((None, bm, D), lambda i: (0, i, 0)),
                  pl.BlockSpec((None, bm, D), lambda i: (1, i, 0)),
                  row, row, row,
                  pl.BlockSpec((8, D), lambda i: (0, 0)),
                  pl.BlockSpec((D, D), lambda i: (0, 0)),
                  pl.BlockSpec((D, D), lambda i: (0, 0))],
        out_specs=[row, row],
        out_shape=[jax.ShapeDtypeStruct((Tp, D), F32), jax.ShapeDtypeStruct((Tp, D), BF16)],
        compiler_params=_cparams("parallel"),
    )(y2, y2, g, bonus, res, vecs, avg, w_o.astype(BF16))


def _router_kernel(x_ref, wh_ref, wl_ref, b_ref, grow_ref, dcol_ref, drow_ref, cnt_ref):
    x = x_ref[...]
    xh, xl = _split2(x)
    wh = wh_ref[...]
    logits = _dot(xh, wh) + _dot(xh, wl_ref[...]) + _dot(xl, wh) + b_ref[...]
    lane = lax.broadcasted_iota(jnp.int32, logits.shape, 1)
    mx = jnp.max(logits, axis=-1, keepdims=True)
    ex = jnp.exp(logits - mx)
    probs = ex / jnp.sum(ex, axis=-1, keepdims=True)
    big = jnp.int32(LANES)
    m1 = jnp.max(probs, axis=-1, keepdims=True)
    i1 = jnp.min(jnp.where(probs == m1, lane, big), axis=-1, keepdims=True)
    sel1 = lane == i1
    rest = jnp.where(sel1 | (lane >= N_EXPERTS), -1.0, probs)
    m2 = jnp.max(rest, axis=-1, keepdims=True)
    i2 = jnp.min(jnp.where(rest == m2, lane, big), axis=-1, keepdims=True)
    sel2 = lane == i2
    tot = m1 + m2
    gates = jnp.where(sel1, m1 / tot, 0.0) + jnp.where(sel2, m2 / tot, 0.0)
    grow_ref[...] = gates.T

    tm = gates.shape[0]
    sel = jnp.where(gates > 0.0, 1.0, 0.0).astype(BF16)
    ti = lax.broadcasted_iota(jnp.int32, (tm, tm), 0)
    tj = lax.broadcasted_iota(jnp.int32, (tm, tm), 1)
    slot_col = _dot(jnp.where(tj < ti, 1.0, 0.0).astype(BF16), sel)
    dcol_ref[...] = jnp.where(gates > 0.0, slot_col, -1.0)
    excl = _dot_tn(sel, jnp.where(ti < tj, 1.0, 0.0).astype(BF16))
    incl = _dot_tn(sel, jnp.where(ti <= tj, 1.0, 0.0).astype(BF16))
    drow_ref[...] = jnp.where(incl - excl > 0.5, excl, -1.0)
    cnt = jnp.sum(jnp.where(gates > 0.0, 1.0, 0.0), axis=0, keepdims=True)
    cnt_ref[...] = jnp.broadcast_to(cnt, cnt_ref.shape).astype(jnp.int32)


def moe_router(h, w_router, b_router, tm):
    Tp, D = h.shape
    E = w_router.shape[1]
    nt = Tp // tm
    wp = jnp.pad(w_router, ((0, 0), (0, LANES - E)))
    wh = wp.astype(BF16)
    wl = (wp - wh.astype(F32)).astype(BF16)
    bp = jnp.concatenate([b_router, jnp.full((LANES - E,), NEG_BIG, F32)]).reshape(1, LANES)
    return pl.pallas_call(
        _router_kernel,
        grid=(nt,),
        in_specs=[pl.BlockSpec((tm, D), lambda i: (i, 0)),
                  pl.BlockSpec((D, LANES), lambda i: (0, 0)),
                  pl.BlockSpec((D, LANES), lambda i: (0, 0)),
                  pl.BlockSpec((1, LANES), lambda i: (0, 0))],
        out_specs=[pl.BlockSpec((None, LANES, tm), lambda i: (i, 0, 0)),
                   pl.BlockSpec((tm, LANES), lambda i: (i, 0)),
                   pl.BlockSpec((None, LANES, tm), lambda i: (i, 0, 0)),
                   pl.BlockSpec((None, 8, LANES), lambda i: (i, 0, 0))],
        out_shape=[jax.ShapeDtypeStruct((nt, LANES, tm), F32),
                   jax.ShapeDtypeStruct((Tp, LANES), F32),
                   jax.ShapeDtypeStruct((nt, LANES, tm), F32),
                   jax.ShapeDtypeStruct((nt, 8, LANES), jnp.int32)],
        compiler_params=_cparams("parallel"),
    )(h, wh, wl, bp)


def _moe_kernel(cnt_ref, x_ref, grow_ref, dcol_ref, drow_ref, wg_ref, wu_ref, wd_ref,
                y_ref, xg_ref, gc_ref, yo_ref, *, tm, n_f):
    i = pl.program_id(0)
    e = pl.program_id(1)
    f = pl.program_id(2)
    nb = (cnt_ref[i, e] + MOE_ROWS - 1) // MOE_ROWS
    npair = (nb + 1) // 2
    PAIR = 2 * MOE_ROWS

    def rows_of(blk):
        return pl.ds(pl.multiple_of(blk * MOE_ROWS, MOE_ROWS), MOE_ROWS)

    def pair_rows(pr):
        return pl.ds(pl.multiple_of(pr * PAIR, PAIR), PAIR)

    @pl.when((e == 0) & (f == 0))
    def _():
        y_ref[...] = jnp.zeros_like(y_ref)

    @pl.when(f == 0)
    def _():
        x = x_ref[...]
        slot = drow_ref[pl.ds(e, 1), :]
        gate = grow_ref[pl.ds(e, 1), :]

        def gather(pr, carry):
            want = (pr * PAIR + lax.broadcasted_iota(jnp.int32, (PAIR, tm), 0)).astype(F32)
            hit = slot == want
            xg_ref[pair_rows(pr), :] = _dot(jnp.where(hit, 1.0, 0.0).astype(BF16), x).astype(BF16)
            gsel = jnp.sum(jnp.where(hit, gate, 0.0), axis=-1, keepdims=True)
            gc_ref[pair_rows(pr), :] = jnp.broadcast_to(gsel, (PAIR, LANES))
            yo_ref[pair_rows(pr), :] = jnp.zeros((PAIR, yo_ref.shape[1]), F32)
            return carry
        lax.fori_loop(0, npair, gather, 0)

    def ffn(blk, carry):
        xb = xg_ref[rows_of(blk), :]
        g = _dot(xb, wg_ref[...])
        u = _dot(xb, wu_ref[...])
        act = (g * jax.nn.sigmoid(g) * u * gc_ref[rows_of(blk), 0:1]).astype(BF16)
        yo_ref[rows_of(blk), :] += _dot(act, wd_ref[...])
        return carry
    lax.fori_loop(0, nb, ffn, 0)

    @pl.when(f == n_f - 1)
    def _():
        lane_t = lax.broadcasted_iota(jnp.int32, (tm, LANES), 1)
        slot = jnp.sum(jnp.where(lane_t == e, dcol_ref[...], 0.0), axis=-1, keepdims=True)

        def scatter(pr, carry):
            have = (pr * PAIR + lax.broadcasted_iota(jnp.int32, (tm, PAIR), 1)).astype(F32)
            put = jnp.where(slot == have, 1.0, 0.0).astype(BF16)
            y_ref[...] += _dot(put, yo_ref[pair_rows(pr), :].astype(BF16))
            return carry
        lax.fori_loop(0, npair, scatter, 0)


def moe_experts(xb, grow, dcol, drow, cnt, wg, wu, wd, tm, fc):
    Tp, D = xb.shape
    E, _, Fe = wg.shape
    n_f = Fe // fc
    nt = Tp // tm
    grid_spec = pltpu.PrefetchScalarGridSpec(
        num_scalar_prefetch=1,
        grid=(nt, E, n_f),
        in_specs=[pl.BlockSpec((tm, D), lambda i, e, f, c: (i, 0)),
                  pl.BlockSpec((None, LANES, tm), lambda i, e, f, c: (i, 0, 0)),
                  pl.BlockSpec((tm, LANES), lambda i, e, f, c: (i, 0)),
                  pl.BlockSpec((None, LANES, tm), lambda i, e, f, c: (i, 0, 0)),
                  pl.BlockSpec((None, D, fc), lambda i, e, f, c: (e, 0, f)),
                  pl.BlockSpec((None, D, fc), lambda i, e, f, c: (e, 0, f)),
                  pl.BlockSpec((None, fc, D), lambda i, e, f, c: (e, f, 0))],
        out_specs=pl.BlockSpec((tm, D), lambda i, e, f, c: (i, 0)),
        scratch_shapes=[pltpu.VMEM((tm + 2 * MOE_ROWS, D), BF16),
                        pltpu.VMEM((tm + 2 * MOE_ROWS, LANES), F32),
                        pltpu.VMEM((tm + 2 * MOE_ROWS, D), F32)],
    )
    return pl.pallas_call(
        functools.partial(_moe_kernel, tm=tm, n_f=n_f),
        grid_spec=grid_spec,
        out_shape=jax.ShapeDtypeStruct((Tp, D), F32),
        compiler_params=_cparams("parallel", "arbitrary", "arbitrary"),
    )(cnt, xb, grow, dcol, drow, wg, wu, wd)


def _res_ln_kernel(y_ref, res_ref, g_ref, b_ref, o_ref):
    o_ref[...] = _layer_norm_rows(DEEPNORM_ALPHA * res_ref[...] + y_ref[...], g_ref[...], b_ref[...])


def res_ln(y, res, g, b, bm):
    M, N = y.shape
    row = pl.BlockSpec((bm, N), lambda i: (i, 0))
    vec = pl.BlockSpec((1, N), lambda i: (0, 0))
    return pl.pallas_call(
        _res_ln_kernel,
        grid=(M // bm,),
        in_specs=[row, row, vec, vec],
        out_specs=row,
        out_shape=jax.ShapeDtypeStruct((M, N), F32),
        compiler_params=_cparams("parallel"),
    )(y, res, g.reshape(1, N), b.reshape(1, N))


def _largest_divisor(n, cap, mult):
    best = mult
    for c in range(mult, cap + 1, mult):
        if n % c == 0:
            best = c
    return best


def kernel(x, meta, ln_g, ln_b, attn_w_in, attn_w_o, attn_lam_q1, attn_lam_k1, attn_lam_q2, attn_lam_k2, attn_subln_g, ffn_w_gate, ffn_w_up, ffn_w_down, rw_mu, rw_w_rkv, rw_w0, rw_w1, rw_w2, rw_a0, rw_a1, rw_a2, rw_g1, rw_g2, rw_k_k, rw_k_a, rw_r_k, rw_lnx_g, rw_lnx_b, rw_w_o, moe_w_router, moe_b_router, moe_w_gate, moe_w_up, moe_w_down):
    B, S, D = x.shape
    L = S + N_META
    Lp = -(-L // SEQ_MULTIPLE) * SEQ_MULTIPLE
    Tp = B * Lp
    BM = 640
    BM_RW = 256

    h = jnp.concatenate([jnp.broadcast_to(meta[None].astype(x.dtype), (B, N_META, D)), x,
                         jnp.zeros((B, Lp - L, D), x.dtype)], axis=1).reshape(Tp, D)

    qkv = matmul(h, attn_w_in[0].astype(BF16), BF16, BM, 1024)
    o = diff_attention_core(qkv, attn_lam_q1[0], attn_lam_k1[0], attn_lam_q2[0], attn_lam_k2[0],
                            attn_subln_g[0], B, Lp, L, 0, bq=640, bk=1280)
    h, hb = matmul_res_ln(o, attn_w_o[0].astype(BF16), h, ln_g[0, 0], ln_b[0, 0], BM, D)
    F = ffn_w_gate.shape[-1]
    act = ffn_up(hb, ffn_w_gate[0].astype(BF16), ffn_w_up[0].astype(BF16), BM,
                 _largest_divisor(F, 1792, LANES))
    h, hb = matmul_res_ln(act, ffn_w_down[0].astype(BF16), h, ln_g[0, 1], ln_b[0, 1], BM,
                          _largest_divisor(F, 2048, LANES))

    r, v, kk, g, bonus, lw, kd, b = rwkv_prep(
        h, rw_mu[0], rw_w_rkv[0], rw_w1[0], rw_w2[0], rw_a1[0], rw_a2[0], rw_g1[0], rw_g2[0],
        rw_w0[0], rw_a0[0], rw_k_k[0], rw_k_a[0], rw_r_k[0], Lp, L, BM_RW)
    y2 = rwkv_scan(r, v, kk, lw, kd, b, B, Lp)
    h, hb = rwkv_post(y2, g, bonus, h, rw_lnx_g[0], rw_lnx_b[0], rw_w_o[0], ln_g[1, 0], ln_b[1, 0],
                      BM_RW)

    grow, dcol, drow, cnt = moe_router(h, moe_w_router[0], moe_b_router[0], MOE_TILE)
    Fe = moe_w_gate.shape[-1]
    E = moe_w_gate.shape[1]
    y = moe_experts(hb, grow, dcol, drow, cnt[:, 0, :E], moe_w_gate[0].astype(BF16),
                    moe_w_up[0].astype(BF16), moe_w_down[0].astype(BF16), MOE_TILE,
                    _largest_divisor(Fe, 1792, LANES))
    h = res_ln(y, h, ln_g[1, 1], ln_b[1, 1], BM)

    return h.reshape(B, Lp, D)[:, N_META:L]
```

```python
import functools
import math

import jax
import jax.numpy as jnp
from jax import lax
from jax.experimental import pallas as pl
from jax.experimental.pallas import tpu as pltpu

F32 = jnp.float32
BF16 = jnp.bfloat16

N_META = 16
DEPTH = 2
LN_EPS = 1e-5
DEEPNORM_ALPHA = (2.0 * DEPTH) ** 0.25

DA_HEADS = 8
DA_HEAD_DIM = 64
DA_V_DIM = 128
SUBLN_EPS = 1e-5

RW_HEAD_DIM = 64
RW_LNX_EPS = 64e-5
RW_CHUNK = 64
RW_GROUP = 256

N_EXPERTS = 8
LANES = 128

SEQ_MULTIPLE = 1280
VMEM_LIMIT = 56 * 1024 * 1024
NEG_BIG = -1e30
LOG2E = 1.4426950408889634
ROWS = 32
STEPS_PER_TRIP = 4
MOE_TILE = 1280
MOE_ROWS = 128


def _cparams(*sem):
    return pltpu.CompilerParams(dimension_semantics=sem, vmem_limit_bytes=VMEM_LIMIT)


def _dot(a, b):
    return jnp.dot(a, b, preferred_element_type=F32)


def _dot_nt(a, b):
    return lax.dot_general(a, b, (((1,), (1,)), ((), ())), preferred_element_type=F32)


def _dot_tn(a, b):
    return lax.dot_general(a, b, (((0,), (0,)), ((), ())), preferred_element_type=F32)


def _split2(x):
    hi = x.astype(BF16)
    lo = (x - hi.astype(F32)).astype(BF16)
    return hi, lo


def _layer_norm_rows(z, g, b):
    mu = jnp.mean(z, -1, keepdims=True)
    zc = z - mu
    var = jnp.mean(zc * zc, -1, keepdims=True)
    return zc * lax.rsqrt(var + LN_EPS) * g + b


def _mm_kernel(a_ref, w_ref, o_ref):
    o_ref[...] = _dot(a_ref[...].astype(BF16), w_ref[...]).astype(o_ref.dtype)


def matmul(a, w, out_dtype, bm, bn):
    M, K = a.shape
    N = w.shape[1]
    return pl.pallas_call(
        _mm_kernel,
        grid=(N // bn, M // bm),
        in_specs=[pl.BlockSpec((bm, K), lambda j, i: (i, 0)),
                  pl.BlockSpec((K, bn), lambda j, i: (0, j))],
        out_specs=pl.BlockSpec((bm, bn), lambda j, i: (i, j)),
        out_shape=jax.ShapeDtypeStruct((M, N), out_dtype),
        compiler_params=_cparams("parallel", "parallel"),
    )(a, w)


def _mm_res_ln_kernel(a_ref, w_ref, res_ref, g_ref, b_ref, o_ref, ob_ref, acc_ref):
    k = pl.program_id(1)

    @pl.when(k == 0)
    def _():
        acc_ref[...] = jnp.zeros_like(acc_ref)

    acc_ref[...] += _dot(a_ref[...].astype(BF16), w_ref[...])

    @pl.when(k == pl.num_programs(1) - 1)
    def _():
        z = DEEPNORM_ALPHA * res_ref[...] + acc_ref[...]
        o = _layer_norm_rows(z, g_ref[...], b_ref[...])
        o_ref[...] = o
        ob_ref[...] = o.astype(BF16)


def matmul_res_ln(a, w, res, g, b, bm, bk):
    M, K = a.shape
    N = w.shape[1]
    return pl.pallas_call(
        _mm_res_ln_kernel,
        grid=(M // bm, K // bk),
        in_specs=[pl.BlockSpec((bm, bk), lambda i, k: (i, k)),
                  pl.BlockSpec((bk, N), lambda i, k: (k, 0)),
                  pl.BlockSpec((bm, N), lambda i, k: (i, 0)),
                  pl.BlockSpec((1, N), lambda i, k: (0, 0)),
                  pl.BlockSpec((1, N), lambda i, k: (0, 0))],
        out_specs=[pl.BlockSpec((bm, N), lambda i, k: (i, 0)),
                   pl.BlockSpec((bm, N), lambda i, k: (i, 0))],
        out_shape=[jax.ShapeDtypeStruct((M, N), F32), jax.ShapeDtypeStruct((M, N), BF16)],
        scratch_shapes=[pltpu.VMEM((bm, N), F32)],
        compiler_params=_cparams("parallel", "arbitrary"),
    )(a, w, res, g.reshape(1, N), b.reshape(1, N))


def _ffn_up_kernel(x_ref, wg_ref, wu_ref, o_ref):
    x = x_ref[...]
    g = _dot(x, wg_ref[...])
    u = _dot(x, wu_ref[...])
    o_ref[...] = (g * jax.nn.sigmoid(g) * u).astype(o_ref.dtype)


def ffn_up(xb, wg, wu, bm, bn):
    M, D = xb.shape
    F = wg.shape[1]
    return pl.pallas_call(
        _ffn_up_kernel,
        grid=(F // bn, M // bm),
        in_specs=[pl.BlockSpec((bm, D), lambda n, i: (i, 0)),
                  pl.BlockSpec((D, bn), lambda n, i: (0, n)),
                  pl.BlockSpec((D, bn), lambda n, i: (0, n))],
        out_specs=pl.BlockSpec((bm, bn), lambda n, i: (i, n)),
        out_shape=jax.ShapeDtypeStruct((M, F), BF16),
        compiler_params=_cparams("parallel", "parallel"),
    )(xb, wg, wu)


def _attn_kernel(slopes_ref, q_ref, k_ref, v_ref, lq1_ref, lk1_ref, lq2_ref, lk2_ref, sg_ref,
                 o_ref, va_ref, qm_ref, s0_ref, s1_ref, p0_ref, p1_ref, m_ref, al0_ref, al1_ref,
                 sh0_ref, sh1_ref, cb0_ref, cb1_ref, acc_ref,
                 *, bq, bk, nk, seq_len, lam_init):
    hd = DA_HEAD_DIM
    s_ref, p_ref, al_ref = (s0_ref, s1_ref), (p0_ref, p1_ref), (al0_ref, al1_ref)
    sh_ref, cb_ref = (sh0_ref, sh1_ref), (cb0_ref, cb1_ref)
    h = pl.program_id(1)
    qi = pl.program_id(2)
    cl = slopes_ref[h] * LOG2E
    ncc = bk // LANES

    @pl.when(qi == 0)
    def _():
        def build(jb, carry):
            off = pl.multiple_of(jb * bk, bk)
            lane = lax.broadcasted_iota(jnp.int32, (bk, DA_V_DIM), 1)
            va_ref[pl.ds(off, bk), 0:DA_V_DIM] = v_ref[pl.ds(off, bk), :]
            va_ref[pl.ds(off, bk), DA_V_DIM:] = jnp.where(lane == 0, 1.0, 0.0).astype(BF16)
            return carry
        lax.fori_loop(0, nk, build, 0)

    q = q_ref[...].astype(F32) * (hd ** -0.5 * LOG2E)
    lane = lax.broadcasted_iota(jnp.int32, (bq, 2 * hd), 1)
    qm_ref[0] = jnp.where(lane < hd, q, 0.0).astype(BF16)
    qm_ref[1] = jnp.where(lane >= hd, q, 0.0).astype(BF16)

    m_ref[...] = jnp.full(m_ref.shape, NEG_BIG, F32)
    acc_ref[...] = jnp.zeros(acc_ref.shape, F32)
    q0 = qi * bq
    n_left = q0 // bk
    first_right = (q0 + bq + bk - 1) // bk
    n_over = first_right - n_left
    n_other = nk - n_over

    def bias_pass(slot, off):
        for r in range(bq // ROWS):
            rows = slice(r * ROWS, (r + 1) * ROWS)
            qpos = q0 + r * ROWS + lax.broadcasted_iota(jnp.int32, (ROWS, LANES), 0)
            kpos = off + lax.broadcasted_iota(jnp.int32, (ROWS, LANES), 1)
            for cc in range(ncc):
                cols = slice(cc * LANES, (cc + 1) * LANES)
                bias = cl * jnp.abs(qpos - (kpos + cc * LANES)).astype(F32)
                for c in range(2):
                    s_ref[slot][c, rows, cols] = s_ref[slot][c, rows, cols] - bias

    def col_bias(slot, off, side):
        jr = lax.broadcasted_iota(jnp.int32, (8, bk), 1)
        cb = jnp.where(off + jr < seq_len, 0.0, -NEG_BIG)
        if side is not None:
            cb = cb + cl * jnp.where(side == 1, jr, bk - 1 - jr).astype(F32)
        cb_ref[slot][...] = cb

    def max_pass(slot, row_slope, row_const):
        for r in range(bq // ROWS):
            rows = slice(r * ROWS, (r + 1) * ROWS)
            ir = (r * ROWS + lax.broadcasted_iota(jnp.int32, (ROWS, LANES), 0)).astype(F32)
            roff = row_slope * ir - row_const
            for c in range(2):
                mx = None
                for cc in range(ncc):
                    cols = slice(cc * LANES, (cc + 1) * LANES)
                    d = s_ref[slot][c, rows, cols] - cb_ref[slot][0:1, cols]
                    s_ref[slot][c, rows, cols] = d
                    mx = d if mx is None else jnp.maximum(mx, d)
                mx = jnp.max(mx, axis=-1, keepdims=True)
                m_old = m_ref[c, rows, :]
                m_new = jnp.maximum(m_old, mx + roff)
                m_ref[c, rows, :] = m_new
                al_ref[slot][c, rows, :] = jnp.exp2(m_old - m_new)
                sh_ref[slot][c, rows, :] = m_new - roff

    def exp_pass(slot):
        for r in range(bq // ROWS):
            rows = slice(r * ROWS, (r + 1) * ROWS)
            for c in range(2):
                shift = sh_ref[slot][c, rows, :]
                for cc in range(ncc):
                    cols = slice(cc * LANES, (cc + 1) * LANES)
                    x = s_ref[slot][c, rows, cols] - shift
                    p_ref[slot][c, rows, cols] = jnp.exp2(x.astype(BF16))

    def rescale(slot):
        for c in range(2):
            al = al_ref[slot][c]
            acc_ref[c] = jnp.concatenate([al, al], axis=1) * acc_ref[c]

    def pv_block(slot, off):
        vb = va_ref[pl.ds(off, bk), :]
        for c in range(2):
            acc_ref[c] += _dot(p_ref[slot][c], vb)

    def block_of(t):
        t = jnp.clip(t, 0, jnp.maximum(n_other - 1, 0))
        j = jnp.minimum(jnp.where(t < n_left, t, t + n_over), nk - 1)
        off = pl.multiple_of(j * bk, bk)
        side = jnp.where(j < n_left, 0, 1)
        return off, side

    def qk_max_block(slot, t):
        off, side = block_of(t)
        for c in range(2):
            s_ref[slot][c] = _dot_nt(qm_ref[c], k_ref[pl.ds(off, bk), :])
        col_bias(slot, off, side)
        gap = jnp.where(side == 1, off - q0, q0 - off - (bk - 1)).astype(F32)
        max_pass(slot, jnp.where(side == 1, cl, -cl), cl * gap)

    off_o = pl.multiple_of(n_left * bk, bk)
    for c in range(2):
        s_ref[1][c] = _dot_nt(qm_ref[c], k_ref[pl.ds(off_o, bk), :])
    bias_pass(1, off_o)
    col_bias(1, off_o, None)
    max_pass(1, 0.0, 0.0)
    qk_max_block(0, 0)
    exp_pass(1)
    rescale(1)
    pv_block(1, off_o)

    def step(t, slot):
        qk_max_block(1 - slot, t + 1)
        exp_pass(slot)
        rescale(slot)
        pv_block(slot, block_of(t)[0])

    def group(u, carry):
        for n in range(STEPS_PER_TRIP):
            step(STEPS_PER_TRIP * u + n, n % 2)
        return carry

    n_pre = jnp.maximum(n_other - 1, 0)
    lax.fori_loop(0, n_pre // STEPS_PER_TRIP, group, 0)

    for n in range(STEPS_PER_TRIP - 1):
        @pl.when(n_pre % STEPS_PER_TRIP > n)
        def _():
            step(n_pre - n_pre % STEPS_PER_TRIP + n, n % 2)

    for slot in range(2):
        @pl.when((n_other > 0) & (n_pre % 2 == slot))
        def _():
            exp_pass(slot)
            rescale(slot)
            pv_block(slot, block_of(n_pre)[0])

    lam = (jnp.exp(jnp.sum(lq1_ref[...] * lk1_ref[...], axis=-1, keepdims=True))
           - jnp.exp(jnp.sum(lq2_ref[...] * lk2_ref[...], axis=-1, keepdims=True)) + lam_init)
    l1 = acc_ref[0, :, DA_V_DIM:DA_V_DIM + 1]
    l2 = acc_ref[1, :, DA_V_DIM:DA_V_DIM + 1]
    o = acc_ref[0, :, 0:DA_V_DIM] / l1 - lam * (acc_ref[1, :, 0:DA_V_DIM] / l2)
    o = o * lax.rsqrt(jnp.mean(o * o, -1, keepdims=True) + SUBLN_EPS) * sg_ref[...] * (1.0 - lam_init)
    o_ref[...] = o.astype(o_ref.dtype)


def diff_attention_core(qkv, lq1, lk1, lq2, lk2, subln_g, B, Lp, seq_len, layer_idx, bq, bk):
    Tp, D3 = qkv.shape
    D = D3 // 3
    H = DA_HEADS
    assert Lp % bk == 0 and bk % bq == 0 and bq % ROWS == 0 and bk % LANES == 0
    nq = Lp // bq
    nk = Lp // bk
    lam_init = 0.8 - 0.6 * math.exp(-0.3 * layer_idx)
    slopes = jnp.asarray([2.0 ** (-(8.0 / H) * (i + 1)) for i in range(H)], F32)
    hd = DA_HEAD_DIM
    grid_spec = pltpu.PrefetchScalarGridSpec(
        num_scalar_prefetch=1,
        grid=(B, H, nq),
        in_specs=[pl.BlockSpec((bq, 2 * hd), lambda b, h, i, s: (b * nq + i, h)),
                  pl.BlockSpec((Lp, 2 * hd), lambda b, h, i, s: (b, H + h),
                               pipeline_mode=pl.Buffered(1)),
                  pl.BlockSpec((Lp, DA_V_DIM), lambda b, h, i, s: (b, 2 * H + h),
                               pipeline_mode=pl.Buffered(1)),
                  pl.BlockSpec((1, hd), lambda b, h, i, s: (0, 0)),
                  pl.BlockSpec((1, hd), lambda b, h, i, s: (0, 0)),
                  pl.BlockSpec((1, hd), lambda b, h, i, s: (0, 0)),
                  pl.BlockSpec((1, hd), lambda b, h, i, s: (0, 0)),
                  pl.BlockSpec((1, DA_V_DIM), lambda b, h, i, s: (0, 0))],
        out_specs=pl.BlockSpec((bq, DA_V_DIM), lambda b, h, i, s: (b * nq + i, h)),
        scratch_shapes=[pltpu.VMEM((Lp, 2 * DA_V_DIM), BF16),
                        pltpu.VMEM((2, bq, 2 * hd), BF16),
                        pltpu.VMEM((2, bq, bk), F32),
                        pltpu.VMEM((2, bq, bk), F32),
                        pltpu.VMEM((2, bq, bk), BF16),
                        pltpu.VMEM((2, bq, bk), BF16),
                        pltpu.VMEM((2, bq, LANES), F32),
                        pltpu.VMEM((2, bq, LANES), F32),
                        pltpu.VMEM((2, bq, LANES), F32),
                        pltpu.VMEM((2, bq, LANES), F32),
                        pltpu.VMEM((2, bq, LANES), F32),
                        pltpu.VMEM((8, bk), F32),
                        pltpu.VMEM((8, bk), F32),
                        pltpu.VMEM((2, bq, 2 * DA_V_DIM), F32)],
    )
    return pl.pallas_call(
        functools.partial(_attn_kernel, bq=bq, bk=bk, nk=nk, seq_len=seq_len, lam_init=lam_init),
        grid_spec=grid_spec,
        out_shape=jax.ShapeDtypeStruct((Tp, D), BF16),
        compiler_params=_cparams("parallel", "parallel", "arbitrary"),
    )(slopes, qkv, qkv, qkv, lq1.reshape(1, hd), lk1.reshape(1, hd), lq2.reshape(1, hd),
      lk2.reshape(1, hd), subln_g.reshape(1, DA_V_DIM))


def _seg_sum(x, member):
    hi, lo = _split2(x)
    per_head = _dot(hi, member) + _dot(lo, member)
    hi, lo = _split2(per_head)
    return _dot_nt(hi, member) + _dot_nt(lo, member)


def _rw_prep_kernel(x_ref, xp_ref, xn_ref, mu_ref, wrkv_ref, w1_ref, w2_ref, a1_ref, a2_ref,
                    g1_ref, g2_ref, vec_ref, ones_ref,
                    r_ref, v_ref, kk_ref, g_ref, bonus_ref, lw_ref, kd_ref, b_ref,
                    *, bm, Lp, seq_len, lora_w, lora_a):
    i = pl.program_id(0)
    x = x_ref[...]
    row = lax.broadcasted_iota(jnp.int32, (bm, 1), 0)
    pos = (i * bm) % Lp + row
    prev_last = xp_ref[7:8, :]
    next_first = xn_ref[0:1, :]
    x_prev = jnp.where(row == 0, prev_last, pltpu.roll(x, 1, 0))
    x_next = jnp.where(row == bm - 1, next_first, pltpu.roll(x, bm - 1, 0))
    x_prev = jnp.where(pos == 0, 0.0, x_prev)
    x_next = jnp.where(pos == seq_len - 1, 0.0, x_next)
    dxp = x_prev - x
    dxn = x_next - x
    valid = pos < seq_len

    def mix(n):
        return (x + mu_ref[n:n + 1, :] * dxp + mu_ref[6 + n:7 + n, :] * dxn).astype(BF16)

    r = _dot(mix(0), wrkv_ref[0])
    k = _dot(mix(1), wrkv_ref[1])
    v = _dot(mix(2), wrkv_ref[2])

    lane = lax.broadcasted_iota(jnp.int32, (bm, 2 * lora_w), 1)
    th = jnp.tanh(_dot(mix(3), w1_ref[...]))
    lw = [_dot(jnp.where((lane >= n * lora_w) & (lane < (n + 1) * lora_w), th, 0.0).astype(BF16),
               w2_ref[...]) for n in range(2)]
    lane = lax.broadcasted_iota(jnp.int32, (bm, 2 * lora_a), 1)
    ah = _dot(mix(4), a1_ref[...])
    la = [_dot(jnp.where((lane >= n * lora_a) & (lane < (n + 1) * lora_a), ah, 0.0).astype(BF16),
               a2_ref[...]) for n in range(2)]
    g = _dot(jax.nn.sigmoid(_dot(mix(5), g1_ref[...])).astype(BF16), g2_ref[...])

    k_k = vec_ref[4:5, :]
    k_a = vec_ref[5:6, :]
    r_k = vec_ref[6:7, :]
    ones_bd = ones_ref[...]

    kkr = k * k_k
    nrm = jnp.sqrt(_seg_sum(kkr * kkr, ones_bd))
    kk = kkr / jnp.maximum(nrm, 1e-12)
    kk = jnp.where(valid, kk, 0.0)
    vz = jnp.where(valid, v, 0.0)

    kd_sum = jnp.zeros_like(k)
    for n in range(2):
        z = -(vec_ref[n:n + 1, :] + lw[n])
        softplus = jnp.maximum(z, 0.0) + jnp.log(1.0 + jnp.exp(-jnp.abs(z)))
        w_log = -softplus - 0.5
        lw_ref[n] = jnp.where(valid, -jnp.exp(w_log), 0.0)
        a = jax.nn.sigmoid(vec_ref[2 + n:3 + n, :] + la[n])
        kd = k * (1.0 + (a - 1.0) * k_a)
        kd_sum = kd_sum + kd
        kd_ref[n] = jnp.where(valid, kd, 0.0)
        b_ref[n] = kk * a

    bonus = _seg_sum(r * kd_sum * r_k, ones_bd) * v
    r_ref[...] = r
    v_ref[...] = vz
    kk_ref[...] = kk
    g_ref[...] = g
    bonus_ref[...] = bonus


def rwkv_prep(h, mu, w_rkv, w1, w2, a1, a2, g1, g2, w0, a0, k_k, k_a, r_k, Lp, seq_len, bm):
    Tp, D = h.shape
    lora_w = w1.shape[-1]
    lora_a = a1.shape[-1]
    lora_g = g1.shape[-1]
    gpad = -(-lora_g // LANES) * LANES
    mu12 = mu.reshape(12, D)
    w1c = jnp.concatenate([w1[0], w1[1]], axis=1).astype(BF16)
    w2c = jnp.concatenate([w2[0], w2[1]], axis=0).astype(BF16)
    a1c = jnp.concatenate([a1[0], a1[1]], axis=1).astype(BF16)
    a2c = jnp.concatenate([a2[0], a2[1]], axis=0).astype(BF16)
    g1p = jnp.pad(g1, ((0, 0), (0, gpad - lora_g))).astype(BF16)
    g2p = jnp.pad(g2, ((0, gpad - lora_g), (0, 0))).astype(BF16)
    vecs = jnp.concatenate([w0, a0, k_k.reshape(1, D), k_a.reshape(1, D), r_k.reshape(1, D),
                            jnp.zeros((1, D), F32)], axis=0)
    seg = jnp.arange(D) // RW_HEAD_DIM
    ones_bd = (seg[:, None] == jnp.arange(LANES)[None, :]).astype(BF16)
    nb8 = bm // 8
    last8 = Tp // 8 - 1
    const2 = lambda i: (0, 0)
    const3 = lambda i: (0, 0, 0)
    row = pl.BlockSpec((bm, D), lambda i: (i, 0))
    row2 = pl.BlockSpec((2, bm, D), lambda i: (0, i, 0))
    sds = jax.ShapeDtypeStruct((Tp, D), F32)
    sds2 = jax.ShapeDtypeStruct((2, Tp, D), F32)
    return pl.pallas_call(
        functools.partial(_rw_prep_kernel, bm=bm, Lp=Lp, seq_len=seq_len, lora_w=lora_w, lora_a=lora_a),
        grid=(Tp // bm,),
        in_specs=[row,
                  pl.BlockSpec((8, D), lambda i: (jnp.maximum(i * nb8 - 1, 0), 0)),
                  pl.BlockSpec((8, D), lambda i: (jnp.minimum((i + 1) * nb8, last8), 0)),
                  pl.BlockSpec((12, D), const2),
                  pl.BlockSpec((3, D, D), const3),
                  pl.BlockSpec((D, 2 * lora_w), const2),
                  pl.BlockSpec((2 * lora_w, D), const2),
                  pl.BlockSpec((D, 2 * lora_a), const2),
                  pl.BlockSpec((2 * lora_a, D), const2),
                  pl.BlockSpec((D, gpad), const2),
                  pl.BlockSpec((gpad, D), const2),
                  pl.BlockSpec((8, D), const2),
                  pl.BlockSpec((D, LANES), const2)],
        out_specs=[row, row, row, row, row, row2, row2, row2],
        out_shape=[sds, sds, sds, sds, sds, sds2, sds2, sds2],
        compiler_params=_cparams("parallel"),
    )(h, h, h, mu12, w_rkv.astype(BF16), w1c, w2c, a1c, a2c, g1p, g2p, vecs, ones_bd)


def _rw_scan_kernel(r_ref, v_ref, kk_ref, lw_ref, kd_ref, b_ref, y_ref, s_ref, *, C, D):
    d = pl.program_id(1)
    c = pl.program_id(2)
    G_ = RW_GROUP
    hpg = G_ // C
    fwd = d == 0

    @pl.when(c == 0)
    def _():
        s_ref[...] = jnp.zeros_like(s_ref)

    sgn = jnp.where(fwd, 1, -1)
    ri = lax.broadcasted_iota(jnp.int32, (C, C), 0)
    ci = lax.broadcasted_iota(jnp.int32, (C, C), 1)
    tri = jnp.where((ri - ci) * sgn >= 0, 1.0, 0.0).astype(BF16)

    lw = lw_ref[...]
    hi = lw.astype(BF16)
    r1 = lw - hi.astype(F32)
    mid = r1.astype(BF16)
    lo = (r1 - mid.astype(F32)).astype(BF16)
    G = _dot(tri, hi) + _dot(tri, mid) + _dot(tri, lo)
    g_end = jnp.where(fwd, G[C - 1:C, :], G[0:1, :])
    e_pos = jnp.exp(G)
    e_neg = jnp.exp(-G)
    e_exc = jnp.exp(G - lw)
    e_rem = jnp.exp(g_end - G)
    kk = kk_ref[...]
    b = b_ref[...]
    kd = kd_ref[...]
    At = (-kk * e_exc).astype(BF16)
    Bt = (b * e_neg).astype(BF16)
    Kt = (kd * e_neg).astype(BF16)
    Rt = (r_ref[...] * e_pos).astype(BF16)
    Bh = (b * e_rem).astype(BF16)
    Kh = (kd * e_rem).astype(BF16)
    Vb = v_ref[...].astype(BF16)
    gam = jnp.exp(g_end)

    R = lax.broadcasted_iota(jnp.int32, (G_, G_), 0)
    Cc = lax.broadcasted_iota(jnp.int32, (G_, G_), 1)
    blk = (R // C) == (Cc // RW_HEAD_DIM)
    dt = jnp.where((R // C) == (Cc // C), (R % C - Cc % C) * sgn, -1)
    strict = dt > 0
    incl = dt >= 0
    eye = jnp.where(R == Cc, 1.0, 0.0)
    zero_b = jnp.zeros((G_, G_), BF16)

    def expand(x, q):
        xq = x[:, q * G_:(q + 1) * G_]
        return jnp.where(blk, jnp.concatenate([xq] * hpg, axis=0), zero_b)

    groups = range(D // G_)
    Ae = [expand(At, q) for q in groups]
    Be = [expand(Bt, q) for q in groups]
    Ke = [expand(Kt, q) for q in groups]
    Re = [expand(Rt, q) for q in groups]
    Ve = [expand(Vb, q) for q in groups]
    Sb = [s_ref[q].astype(BF16) for q in groups]
    N = [jnp.where(strict, _dot_nt(Ae[q], Be[q]), 0.0) for q in groups]
    P = [eye + N[q] for q in groups]
    Xb = [N[q].astype(BF16) for q in groups]
    Aak = [jnp.where(strict, _dot_nt(Ae[q], Ke[q]), 0.0).astype(BF16) for q in groups]
    rhs = [(_dot_nt(Ae[q], Sb[q]) + _dot(Aak[q], Ve[q])).astype(BF16) for q in groups]
    p = 1
    while 2 * p < C:
        Xb = [_dot(Xb[q], Xb[q]).astype(BF16) for q in groups]
        P = [P[q] + _dot(P[q].astype(BF16), Xb[q]) for q in groups]
        p *= 2
    Ub = [_dot(P[q].astype(BF16), rhs[q]).astype(BF16) for q in groups]
    Arb = [jnp.where(incl, _dot_nt(Re[q], Be[q]), 0.0).astype(BF16) for q in groups]
    Ark = [jnp.where(incl, _dot_nt(Re[q], Ke[q]), 0.0).astype(BF16) for q in groups]
    for q in groups:
        Y = _dot_nt(Re[q], Sb[q]) + _dot(Arb[q], Ub[q]) + _dot(Ark[q], Ve[q])
        yq = Y[0:C, :]
        for hb in range(1, hpg):
            yq = yq + Y[hb * C:(hb + 1) * C, :]
        y_ref[:, q * G_:(q + 1) * G_] = yq
    for q in groups:
        s_ref[q] = (s_ref[q] * gam[:, q * G_:(q + 1) * G_] + _dot_tn(Ub[q], expand(Bh, q))
                    + _dot_tn(Ve[q], expand(Kh, q)))


def rwkv_scan(r, v, kk, lw, kd, b, B, Lp):
    Tp, D = r.shape
    C = RW_CHUNK
    nch = Lp // C

    def rb(bi, d, c):
        return bi * nch + jnp.where(d == 0, c, nch - 1 - c)

    shared = pl.BlockSpec((C, D), lambda bi, d, c: (rb(bi, d, c), 0))
    perdir = pl.BlockSpec((None, C, D), lambda bi, d, c: (d, rb(bi, d, c), 0))
    return pl.pallas_call(
        functools.partial(_rw_scan_kernel, C=C, D=D),
        grid=(B, 2, nch),
        in_specs=[shared, shared, shared, perdir, perdir, perdir],
        out_specs=perdir,
        out_shape=jax.ShapeDtypeStruct((2, Tp, D), F32),
        scratch_shapes=[pltpu.VMEM((D // RW_GROUP, RW_GROUP, RW_GROUP), F32)],
        compiler_params=_cparams("parallel", "parallel", "arbitrary"),
    )(r, v, kk, lw, kd, b)


def _rw_post_kernel(yf_ref, yb_ref, g_ref, bonus_ref, res_ref, vec_ref, avg_ref, wo_ref,
                    o_ref, ob_ref):
    avg = avg_ref[...]
    y = yf_ref[...] + yb_ref[...]
    ym = _seg_sum(y, avg)
    yc = y - ym
    yv = _seg_sum(yc * yc, avg)
    yn = yc * lax.rsqrt(yv + RW_LNX_EPS) * vec_ref[0:1, :] + vec_ref[1:2, :]
    out = ((yn + bonus_ref[...]) * g_ref[...]).astype(BF16)
    z = DEEPNORM_ALPHA * res_ref[...] + _dot(out, wo_ref[...])
    o = _layer_norm_rows(z, vec_ref[2:3, :], vec_ref[3:4, :])
    o_ref[...] = o
    ob_ref[...] = o.astype(BF16)


def rwkv_post(y2, g, bonus, res, lnx_g, lnx_b, w_o, ln_g, ln_b, bm):
    Tp, D = res.shape
    vecs = jnp.concatenate([lnx_g.reshape(1, D), lnx_b.reshape(1, D), ln_g.reshape(1, D),
                            ln_b.reshape(1, D), jnp.zeros((4, D), F32)], axis=0)
    seg = jnp.arange(D) // RW_HEAD_DIM
    avg = ((seg[:, None] == jnp.arange(LANES)[None, :]).astype(F32) * RW_HEAD_DIM ** -0.5).astype(BF16)
    row = pl.BlockSpec((bm, D), lambda i: (i, 0))
    return pl.pallas_call(
        _rw_post_kernel,
        grid=(Tp // bm,),
        in_specs=[pl.BlockSpec((None, bm, D), lambda i: (0, i, 0)),
                  pl.BlockSpec((None, bm, D), lambda i: (1, i, 0)),
                  row, row, row,
                  pl.BlockSpec((8, D), lambda i: (0, 0)),
                  pl.BlockSpec((D, LANES), lambda i: (0, 0)),
                  pl.BlockSpec((D, D), lambda i: (0, 0))],
        out_specs=[row, row],
        out_shape=[jax.ShapeDtypeStruct((Tp, D), F32), jax.ShapeDtypeStruct((Tp, D), BF16)],
        compiler_params=_cparams("parallel"),
    )(y2, y2, g, bonus, res, vecs, avg, w_o.astype(BF16))


def _router_kernel(x_ref, wh_ref, wl_ref, b_ref, grow_ref, dcol_ref, drow_ref, cnt_ref):
    x = x_ref[...]
    xh, xl = _split2(x)
    wh = wh_ref[...]
    logits = _dot(xh, wh) + _dot(xh, wl_ref[...]) + _dot(xl, wh) + b_ref[...]
    lane = lax.broadcasted_iota(jnp.int32, logits.shape, 1)
    mx = jnp.max(logits, axis=-1, keepdims=True)
    ex = jnp.exp(logits - mx)
    probs = ex / jnp.sum(ex, axis=-1, keepdims=True)
    big = jnp.int32(LANES)
    m1 = jnp.max(probs, axis=-1, keepdims=True)
    i1 = jnp.min(jnp.where(probs == m1, lane, big), axis=-1, keepdims=True)
    sel1 = lane == i1
    rest = jnp.where(sel1 | (lane >= N_EXPERTS), -1.0, probs)
    m2 = jnp.max(rest, axis=-1, keepdims=True)
    i2 = jnp.min(jnp.where(rest == m2, lane, big), axis=-1, keepdims=True)
    sel2 = lane == i2
    tot = m1 + m2
    gates = jnp.where(sel1, m1 / tot, 0.0) + jnp.where(sel2, m2 / tot, 0.0)
    grow_ref[...] = gates.T

    tm = gates.shape[0]
    sel = jnp.where(gates > 0.0, 1.0, 0.0).astype(BF16)
    ti = lax.broadcasted_iota(jnp.int32, (tm, tm), 0)
    tj = lax.broadcasted_iota(jnp.int32, (tm, tm), 1)
    slot_col = _dot(jnp.where(tj < ti, 1.0, 0.0).astype(BF16), sel)
    dcol_ref[...] = jnp.where(gates > 0.0, slot_col, -1.0)
    excl = _dot_tn(sel, jnp.where(ti < tj, 1.0, 0.0).astype(BF16))
    incl = _dot_tn(sel, jnp.where(ti <= tj, 1.0, 0.0).astype(BF16))
    drow_ref[...] = jnp.where(incl - excl > 0.5, excl, -1.0)
    cnt = jnp.sum(jnp.where(gates > 0.0, 1.0, 0.0), axis=0, keepdims=True)
    cnt_ref[...] = jnp.broadcast_to(cnt, cnt_ref.shape).astype(jnp.int32)


def moe_router(h, w_router, b_router, tm):
    Tp, D = h.shape
    E = w_router.shape[1]
    nt = Tp // tm
    wp = jnp.pad(w_router, ((0, 0), (0, LANES - E)))
    wh = wp.astype(BF16)
    wl = (wp - wh.astype(F32)).astype(BF16)
    bp = jnp.concatenate([b_router, jnp.full((LANES - E,), NEG_BIG, F32)]).reshape(1, LANES)
    return pl.pallas_call(
        _router_kernel,
        grid=(nt,),
        in_specs=[pl.BlockSpec((tm, D), lambda i: (i, 0)),
                  pl.BlockSpec((D, LANES), lambda i: (0, 0)),
                  pl.BlockSpec((D, LANES), lambda i: (0, 0)),
                  pl.BlockSpec((1, LANES), lambda i: (0, 0))],
        out_specs=[pl.BlockSpec((None, LANES, tm), lambda i: (i, 0, 0)),
                   pl.BlockSpec((tm, LANES), lambda i: (i, 0)),
                   pl.BlockSpec((None, LANES, tm), lambda i: (i, 0, 0)),
                   pl.BlockSpec((None, 8, LANES), lambda i: (i, 0, 0))],
        out_shape=[jax.ShapeDtypeStruct((nt, LANES, tm), F32),
                   jax.ShapeDtypeStruct((Tp, LANES), F32),
                   jax.ShapeDtypeStruct((nt, LANES, tm), F32),
                   jax.ShapeDtypeStruct((nt, 8, LANES), jnp.int32)],
        compiler_params=_cparams("parallel"),
    )(h, wh, wl, bp)


def _moe_kernel(cnt_ref, x_ref, grow_ref, dcol_ref, drow_ref, wg_ref, wu_ref, wd_ref,
                y_ref, xg_ref, gc_ref, yo_ref, *, tm, n_f):
    i = pl.program_id(0)
    e = pl.program_id(1)
    f = pl.program_id(2)
    nb = (cnt_ref[i, e] + MOE_ROWS - 1) // MOE_ROWS
    npair = (nb + 1) // 2
    PAIR = 2 * MOE_ROWS

    def rows_of(blk):
        return pl.ds(pl.multiple_of(blk * MOE_ROWS, MOE_ROWS), MOE_ROWS)

    def pair_rows(pr):
        return pl.ds(pl.multiple_of(pr * PAIR, PAIR), PAIR)

    @pl.when((e == 0) & (f == 0))
    def _():
        y_ref[...] = jnp.zeros_like(y_ref)

    @pl.when(f == 0)
    def _():
        x = x_ref[...]
        slot = drow_ref[pl.ds(e, 1), :]
        gate = grow_ref[pl.ds(e, 1), :]

        def gather(pr, carry):
            want = (pr * PAIR + lax.broadcasted_iota(jnp.int32, (PAIR, tm), 0)).astype(F32)
            hit = slot == want
            xg_ref[pair_rows(pr), :] = _dot(jnp.where(hit, 1.0, 0.0).astype(BF16), x).astype(BF16)
            gsel = jnp.sum(jnp.where(hit, gate, 0.0), axis=-1, keepdims=True)
            gc_ref[pair_rows(pr), :] = jnp.broadcast_to(gsel, (PAIR, LANES))
            yo_ref[pair_rows(pr), :] = jnp.zeros((PAIR, yo_ref.shape[1]), F32)
            return carry
        lax.fori_loop(0, npair, gather, 0)

    def ffn(blk, carry):
        xb = xg_ref[rows_of(blk), :]
        g = _dot(xb, wg_ref[...])
        u = _dot(xb, wu_ref[...])
        act = (g * jax.nn.sigmoid(g) * u * gc_ref[rows_of(blk), 0:1]).astype(BF16)
        yo_ref[rows_of(blk), :] += _dot(act, wd_ref[...])
        return carry
    lax.fori_loop(0, nb, ffn, 0)

    @pl.when(f == n_f - 1)
    def _():
        lane_t = lax.broadcasted_iota(jnp.int32, (tm, LANES), 1)
        slot = jnp.sum(jnp.where(lane_t == e, dcol_ref[...], 0.0), axis=-1, keepdims=True)

        def scatter(pr, carry):
            have = (pr * PAIR + lax.broadcasted_iota(jnp.int32, (tm, PAIR), 1)).astype(F32)
            put = jnp.where(slot == have, 1.0, 0.0).astype(BF16)
            y_ref[...] += _dot(put, yo_ref[pair_rows(pr), :].astype(BF16))
            return carry
        lax.fori_loop(0, npair, scatter, 0)


def moe_experts(xb, grow, dcol, drow, cnt, wg, wu, wd, tm, fc):
    Tp, D = xb.shape
    E, _, Fe = wg.shape
    n_f = Fe // fc
    nt = Tp // tm
    grid_spec = pltpu.PrefetchScalarGridSpec(
        num_scalar_prefetch=1,
        grid=(nt, E, n_f),
        in_specs=[pl.BlockSpec((tm, D), lambda i, e, f, c: (i, 0)),
                  pl.BlockSpec((None, LANES, tm), lambda i, e, f, c: (i, 0, 0)),
                  pl.BlockSpec((tm, LANES), lambda i, e, f, c: (i, 0)),
                  pl.BlockSpec((None, LANES, tm), lambda i, e, f, c: (i, 0, 0)),
                  pl.BlockSpec((None, D, fc), lambda i, e, f, c: (e, 0, f)),
                  pl.BlockSpec((None, D, fc), lambda i, e, f, c: (e, 0, f)),
                  pl.BlockSpec((None, fc, D), lambda i, e, f, c: (e, f, 0))],
        out_specs=pl.BlockSpec((tm, D), lambda i, e, f, c: (i, 0)),
        scratch_shapes=[pltpu.VMEM((tm + 2 * MOE_ROWS, D), BF16),
                        pltpu.VMEM((tm + 2 * MOE_ROWS, LANES), F32),
                        pltpu.VMEM((tm + 2 * MOE_ROWS, D), F32)],
    )
    return pl.pallas_call(
        functools.partial(_moe_kernel, tm=tm, n_f=n_f),
        grid_spec=grid_spec,
        out_shape=jax.ShapeDtypeStruct((Tp, D), F32),
        compiler_params=_cparams("parallel", "arbitrary", "arbitrary"),
    )(cnt, xb, grow, dcol, drow, wg, wu, wd)


def _res_ln_kernel(y_ref, res_ref, g_ref, b_ref, o_ref):
    o_ref[...] = _layer_norm_rows(DEEPNORM_ALPHA * res_ref[...] + y_ref[...], g_ref[...], b_ref[...])


def res_ln(y, res, g, b, bm):
    M, N = y.shape
    row = pl.BlockSpec((bm, N), lambda i: (i, 0))
    vec = pl.BlockSpec((1, N), lambda i: (0, 0))
    return pl.pallas_call(
        _res_ln_kernel,
        grid=(M // bm,),
        in_specs=[row, row, vec, vec],
        out_specs=row,
        out_shape=jax.ShapeDtypeStruct((M, N), F32),
        compiler_params=_cparams("parallel"),
    )(y, res, g.reshape(1, N), b.reshape(1, N))


def _largest_divisor(n, cap, mult):
    best = mult
    for c in range(mult, cap + 1, mult):
        if n % c == 0:
            best = c
    return best


def kernel(x, meta, ln_g, ln_b, attn_w_in, attn_w_o, attn_lam_q1, attn_lam_k1, attn_lam_q2, attn_lam_k2, attn_subln_g, ffn_w_gate, ffn_w_up, ffn_w_down, rw_mu, rw_w_rkv, rw_w0, rw_w1, rw_w2, rw_a0, rw_a1, rw_a2, rw_g1, rw_g2, rw_k_k, rw_k_a, rw_r_k, rw_lnx_g, rw_lnx_b, rw_w_o, moe_w_router, moe_b_router, moe_w_gate, moe_w_up, moe_w_down):
    B, S, D = x.shape
    L = S + N_META
    Lp = -(-L // SEQ_MULTIPLE) * SEQ_MULTIPLE
    Tp = B * Lp
    BM = 640
    BM_RW = 256

    h = jnp.concatenate([jnp.broadcast_to(meta[None].astype(x.dtype), (B, N_META, D)), x,
                         jnp.zeros((B, Lp - L, D), x.dtype)], axis=1).reshape(Tp, D)

    qkv = matmul(h, attn_w_in[0].astype(BF16), BF16, BM, 1024)
    o = diff_attention_core(qkv, attn_lam_q1[0], attn_lam_k1[0], attn_lam_q2[0], attn_lam_k2[0],
                            attn_subln_g[0], B, Lp, L, 0, bq=640, bk=1280)
    h, hb = matmul_res_ln(o, attn_w_o[0].astype(BF16), h, ln_g[0, 0], ln_b[0, 0], BM, D)
    F = ffn_w_gate.shape[-1]
    act = ffn_up(hb, ffn_w_gate[0].astype(BF16), ffn_w_up[0].astype(BF16), BM,
                 _largest_divisor(F, 1792, LANES))
    h, hb = matmul_res_ln(act, ffn_w_down[0].astype(BF16), h, ln_g[0, 1], ln_b[0, 1], BM,
                          _largest_divisor(F, 2048, LANES))

    r, v, kk, g, bonus, lw, kd, b = rwkv_prep(
        h, rw_mu[0], rw_w_rkv[0], rw_w1[0], rw_w2[0], rw_a1[0], rw_a2[0], rw_g1[0], rw_g2[0],
        rw_w0[0], rw_a0[0], rw_k_k[0], rw_k_a[0], rw_r_k[0], Lp, L, BM_RW)
    y2 = rwkv_scan(r, v, kk, lw, kd, b, B, Lp)
    h, hb = rwkv_post(y2, g, bonus, h, rw_lnx_g[0], rw_lnx_b[0], rw_w_o[0], ln_g[1, 0], ln_b[1, 0],
                      BM_RW)

    grow, dcol, drow, cnt = moe_router(h, moe_w_router[0], moe_b_router[0], MOE_TILE)
    Fe = moe_w_gate.shape[-1]
    E = moe_w_gate.shape[1]
    y = moe_experts(hb, grow, dcol, drow, cnt[:, 0, :E], moe_w_gate[0].astype(BF16),
                    moe_w_up[0].astype(BF16), moe_w_down[0].astype(BF16), MOE_TILE,
                    _largest_divisor(Fe, 1792, LANES))
    h = res_ln(y, h, ln_g[1, 1], ln_b[1, 1], BM)

    return h.reshape(B, Lp, D)[:, N_META:L]
```

```python
import functools
import math

import jax
import jax.numpy as jnp
from jax import lax
from jax.experimental import pallas as pl
from jax.experimental.pallas import tpu as pltpu

F32 = jnp.float32
BF16 = jnp.bfloat16

N_META = 16
DEPTH = 2
LN_EPS = 1e-5
DEEPNORM_ALPHA = (2.0 * DEPTH) ** 0.25

DA_HEADS = 8
DA_HEAD_DIM = 64
DA_V_DIM = 128
SUBLN_EPS = 1e-5

RW_HEAD_DIM = 64
RW_LNX_EPS = 64e-5
RW_CHUNK = 64
RW_GROUP = 256

N_EXPERTS = 8
LANES = 128

SEQ_MULTIPLE = 1280
VMEM_LIMIT = 56 * 1024 * 1024
NEG_BIG = -1e30
LOG2E = 1.4426950408889634
ROWS = 32
STEPS_PER_TRIP = 4
MOE_TILE = 1280
MOE_ROWS = 128


def _cparams(*sem):
    return pltpu.CompilerParams(dimension_semantics=sem, vmem_limit_bytes=VMEM_LIMIT)


def _dot(a, b):
    return jnp.dot(a, b, preferred_element_type=F32)


def _dot_nt(a, b):
    return lax.dot_general(a, b, (((1,), (1,)), ((), ())), preferred_element_type=F32)


def _dot_tn(a, b):
    return lax.dot_general(a, b, (((0,), (0,)), ((), ())), preferred_element_type=F32)


def _split2(x):
    hi = x.astype(BF16)
    lo = (x - hi.astype(F32)).astype(BF16)
    return hi, lo


def _layer_norm_rows(z, g, b):
    mu = jnp.mean(z, -1, keepdims=True)
    zc = z - mu
    var = jnp.mean(zc * zc, -1, keepdims=True)
    return zc * lax.rsqrt(var + LN_EPS) * g + b


def _mm_kernel(a_ref, w_ref, o_ref):
    o_ref[...] = _dot(a_ref[...].astype(BF16), w_ref[...]).astype(o_ref.dtype)


def matmul(a, w, out_dtype, bm, bn):
    M, K = a.shape
    N = w.shape[1]
    return pl.pallas_call(
        _mm_kernel,
        grid=(N // bn, M // bm),
        in_specs=[pl.BlockSpec((bm, K), lambda j, i: (i, 0)),
                  pl.BlockSpec((K, bn), lambda j, i: (0, j))],
        out_specs=pl.BlockSpec((bm, bn), lambda j, i: (i, j)),
        out_shape=jax.ShapeDtypeStruct((M, N), out_dtype),
        compiler_params=_cparams("parallel", "parallel"),
    )(a, w)


def _mm_res_ln_kernel(a_ref, w_ref, res_ref, g_ref, b_ref, o_ref, ob_ref, acc_ref):
    k = pl.program_id(1)

    @pl.when(k == 0)
    def _():
        acc_ref[...] = jnp.zeros_like(acc_ref)

    acc_ref[...] += _dot(a_ref[...].astype(BF16), w_ref[...])

    @pl.when(k == pl.num_programs(1) - 1)
    def _():
        z = DEEPNORM_ALPHA * res_ref[...] + acc_ref[...]
        o = _layer_norm_rows(z, g_ref[...], b_ref[...])
        o_ref[...] = o
        ob_ref[...] = o.astype(BF16)


def matmul_res_ln(a, w, res, g, b, bm, bk):
    M, K = a.shape
    N = w.shape[1]
    return pl.pallas_call(
        _mm_res_ln_kernel,
        grid=(M // bm, K // bk),
        in_specs=[pl.BlockSpec((bm, bk), lambda i, k: (i, k)),
                  pl.BlockSpec((bk, N), lambda i, k: (k, 0)),
                  pl.BlockSpec((bm, N), lambda i, k: (i, 0)),
                  pl.BlockSpec((1, N), lambda i, k: (0, 0)),
                  pl.BlockSpec((1, N), lambda i, k: (0, 0))],
        out_specs=[pl.BlockSpec((bm, N), lambda i, k: (i, 0)),
                   pl.BlockSpec((bm, N), lambda i, k: (i, 0))],
        out_shape=[jax.ShapeDtypeStruct((M, N), F32), jax.ShapeDtypeStruct((M, N), BF16)],
        scratch_shapes=[pltpu.VMEM((bm, N), F32)],
        compiler_params=_cparams("parallel", "arbitrary"),
    )(a, w, res, g.reshape(1, N), b.reshape(1, N))


def _ffn_up_kernel(x_ref, wg_ref, wu_ref, o_ref):
    x = x_ref[...]
    g = _dot(x, wg_ref[...])
    u = _dot(x, wu_ref[...])
    o_ref[...] = (g * jax.nn.sigmoid(g) * u).astype(o_ref.dtype)


def ffn_up(xb, wg, wu, bm, bn):
    M, D = xb.shape
    F = wg.shape[1]
    return pl.pallas_call(
        _ffn_up_kernel,
        grid=(F // bn, M // bm),
        in_specs=[pl.BlockSpec((bm, D), lambda n, i: (i, 0)),
                  pl.BlockSpec((D, bn), lambda n, i: (0, n)),
                  pl.BlockSpec((D, bn), lambda n, i: (0, n))],
        out_specs=pl.BlockSpec((bm, bn), lambda n, i: (i, n)),
        out_shape=jax.ShapeDtypeStruct((M, F), BF16),
        compiler_params=_cparams("parallel", "parallel"),
    )(xb, wg, wu)


def _attn_kernel(slopes_ref, q_ref, k_ref, v_ref, lq1_ref, lk1_ref, lq2_ref, lk2_ref, sg_ref,
                 o_ref, va_ref, qm_ref, s0_ref, s1_ref, p0_ref, p1_ref, m_ref, al0_ref, al1_ref,
                 sh0_ref, sh1_ref, cb0_ref, cb1_ref, acc_ref,
                 *, bq, bk, nk, seq_len, lam_init):
    hd = DA_HEAD_DIM
    s_ref, p_ref, al_ref = (s0_ref, s1_ref), (p0_ref, p1_ref), (al0_ref, al1_ref)
    sh_ref, cb_ref = (sh0_ref, sh1_ref), (cb0_ref, cb1_ref)
    h = pl.program_id(1)
    qi = pl.program_id(2)
    cl = slopes_ref[h] * LOG2E
    ncc = bk // LANES

    @pl.when(qi == 0)
    def _():
        def build(jb, carry):
            off = pl.multiple_of(jb * bk, bk)
            lane = lax.broadcasted_iota(jnp.int32, (bk, DA_V_DIM), 1)
            va_ref[pl.ds(off, bk), 0:DA_V_DIM] = v_ref[pl.ds(off, bk), :]
            va_ref[pl.ds(off, bk), DA_V_DIM:] = jnp.where(lane == 0, 1.0, 0.0).astype(BF16)
            return carry
        lax.fori_loop(0, nk, build, 0)

    q = q_ref[...].astype(F32) * (hd ** -0.5 * LOG2E)
    lane = lax.broadcasted_iota(jnp.int32, (bq, 2 * hd), 1)
    qm_ref[0] = jnp.where(lane < hd, q, 0.0).astype(BF16)
    qm_ref[1] = jnp.where(lane >= hd, q, 0.0).astype(BF16)

    m_ref[...] = jnp.full(m_ref.shape, NEG_BIG, F32)
    acc_ref[...] = jnp.zeros(acc_ref.shape, F32)
    q0 = qi * bq
    n_left = q0 // bk
    first_right = (q0 + bq + bk - 1) // bk
    n_over = first_right - n_left
    n_other = nk - n_over

    def col_bias(slot, off, side):
        jr = lax.broadcasted_iota(jnp.int32, (8, bk), 1)
        cb = jnp.where(off + jr < seq_len, 0.0, -NEG_BIG)
        if side is not None:
            cb = cb + cl * jnp.where(side == 1, jr, bk - 1 - jr).astype(F32)
        cb_ref[slot][...] = cb

    def max_pass(slot, row_slope, row_const, explicit_off=None):
        for r in range(bq // ROWS):
            rows = slice(r * ROWS, (r + 1) * ROWS)
            ir = (r * ROWS + lax.broadcasted_iota(jnp.int32, (ROWS, LANES), 0)).astype(F32)
            roff = row_slope * ir - row_const
            bias = None
            if explicit_off is not None:
                qpos = q0 + r * ROWS + lax.broadcasted_iota(jnp.int32, (ROWS, LANES), 0)
                kpos = explicit_off + lax.broadcasted_iota(jnp.int32, (ROWS, LANES), 1)
                bias = [cl * jnp.abs(qpos - (kpos + cc * LANES)).astype(F32) for cc in range(ncc)]
            for c in range(2):
                mx = None
                for cc in range(ncc):
                    cols = slice(cc * LANES, (cc + 1) * LANES)
                    d = s_ref[slot][c, rows, cols] - cb_ref[slot][0:1, cols]
                    if bias is not None:
                        d = d - bias[cc]
                    s_ref[slot][c, rows, cols] = d
                    mx = d if mx is None else jnp.maximum(mx, d)
                mx = jnp.max(mx, axis=-1, keepdims=True)
                m_old = m_ref[c, rows, :]
                m_new = jnp.maximum(m_old, mx + roff)
                m_ref[c, rows, :] = m_new
                al_ref[slot][c, rows, :] = jnp.exp2(m_old - m_new)
                sh_ref[slot][c, rows, :] = m_new - roff

    def exp_pass(slot):
        for r in range(bq // ROWS):
            rows = slice(r * ROWS, (r + 1) * ROWS)
            for c in range(2):
                shift = sh_ref[slot][c, rows, :]
                for cc in range(ncc):
                    cols = slice(cc * LANES, (cc + 1) * LANES)
                    x = s_ref[slot][c, rows, cols] - shift
                    p_ref[slot][c, rows, cols] = jnp.exp2(x.astype(BF16))

    def rescale(slot):
        for c in range(2):
            al = al_ref[slot][c]
            acc_ref[c] = jnp.concatenate([al, al], axis=1) * acc_ref[c]

    def pv_block(slot, off):
        vb = va_ref[pl.ds(off, bk), :]
        for c in range(2):
            acc_ref[c] += _dot(p_ref[slot][c], vb)

    def block_of(t):
        t = jnp.clip(t, 0, jnp.maximum(n_other - 1, 0))
        j = jnp.minimum(jnp.where(t < n_left, t, t + n_over), nk - 1)
        off = pl.multiple_of(j * bk, bk)
        side = jnp.where(j < n_left, 0, 1)
        return off, side

    def qk_max_block(slot, t):
        off, side = block_of(t)
        for c in range(2):
            s_ref[slot][c] = _dot_nt(qm_ref[c], k_ref[pl.ds(off, bk), :])
        col_bias(slot, off, side)
        gap = jnp.where(side == 1, off - q0, q0 - off - (bk - 1)).astype(F32)
        max_pass(slot, jnp.where(side == 1, cl, -cl), cl * gap)

    off_o = pl.multiple_of(n_left * bk, bk)
    for c in range(2):
        s_ref[1][c] = _dot_nt(qm_ref[c], k_ref[pl.ds(off_o, bk), :])
    col_bias(1, off_o, None)
    max_pass(1, 0.0, 0.0, explicit_off=off_o)
    qk_max_block(0, 0)
    exp_pass(1)
    rescale(1)
    pv_block(1, off_o)

    def step(t, slot):
        qk_max_block(1 - slot, t + 1)
        exp_pass(slot)
        rescale(slot)
        pv_block(slot, block_of(t)[0])

    def group(u, carry):
        for n in range(STEPS_PER_TRIP):
            step(STEPS_PER_TRIP * u + n, n % 2)
        return carry

    n_pre = jnp.maximum(n_other - 1, 0)
    lax.fori_loop(0, n_pre // STEPS_PER_TRIP, group, 0)

    for n in range(STEPS_PER_TRIP - 1):
        @pl.when(n_pre % STEPS_PER_TRIP > n)
        def _():
            step(n_pre - n_pre % STEPS_PER_TRIP + n, n % 2)

    for slot in range(2):
        @pl.when((n_other > 0) & (n_pre % 2 == slot))
        def _():
            exp_pass(slot)
            rescale(slot)
            pv_block(slot, block_of(n_pre)[0])

    lam = (jnp.exp(jnp.sum(lq1_ref[...] * lk1_ref[...], axis=-1, keepdims=True))
           - jnp.exp(jnp.sum(lq2_ref[...] * lk2_ref[...], axis=-1, keepdims=True)) + lam_init)
    l1 = acc_ref[0, :, DA_V_DIM:DA_V_DIM + 1]
    l2 = acc_ref[1, :, DA_V_DIM:DA_V_DIM + 1]
    o = acc_ref[0, :, 0:DA_V_DIM] / l1 - lam * (acc_ref[1, :, 0:DA_V_DIM] / l2)
    o = o * lax.rsqrt(jnp.mean(o * o, -1, keepdims=True) + SUBLN_EPS) * sg_ref[...] * (1.0 - lam_init)
    o_ref[...] = o.astype(o_ref.dtype)


def diff_attention_core(qkv, lq1, lk1, lq2, lk2, subln_g, B, Lp, seq_len, layer_idx, bq, bk):
    Tp, D3 = qkv.shape
    D = D3 // 3
    H = DA_HEADS
    assert Lp % bk == 0 and bk % bq == 0 and bq % ROWS == 0 and bk % LANES == 0
    nq = Lp // bq
    nk = Lp // bk
    lam_init = 0.8 - 0.6 * math.exp(-0.3 * layer_idx)
    slopes = jnp.asarray([2.0 ** (-(8.0 / H) * (i + 1)) for i in range(H)], F32)
    hd = DA_HEAD_DIM
    grid_spec = pltpu.PrefetchScalarGridSpec(
        num_scalar_prefetch=1,
        grid=(B, H, nq),
        in_specs=[pl.BlockSpec((bq, 2 * hd), lambda b, h, i, s: (b * nq + i, h)),
                  pl.BlockSpec((Lp, 2 * hd), lambda b, h, i, s: (b, H + h),
                               pipeline_mode=pl.Buffered(1)),
                  pl.BlockSpec((Lp, DA_V_DIM), lambda b, h, i, s: (b, 2 * H + h),
                               pipeline_mode=pl.Buffered(1)),
                  pl.BlockSpec((1, hd), lambda b, h, i, s: (0, 0)),
                  pl.BlockSpec((1, hd), lambda b, h, i, s: (0, 0)),
                  pl.BlockSpec((1, hd), lambda b, h, i, s: (0, 0)),
                  pl.BlockSpec((1, hd), lambda b, h, i, s: (0, 0)),
                  pl.BlockSpec((1, DA_V_DIM), lambda b, h, i, s: (0, 0))],
        out_specs=pl.BlockSpec((bq, DA_V_DIM), lambda b, h, i, s: (b * nq + i, h)),
        scratch_shapes=[pltpu.VMEM((Lp, 2 * DA_V_DIM), BF16),
                        pltpu.VMEM((2, bq, 2 * hd), BF16),
                        pltpu.VMEM((2, bq, bk), F32),
                        pltpu.VMEM((2, bq, bk), F32),
                        pltpu.VMEM((2, bq, bk), BF16),
                        pltpu.VMEM((2, bq, bk), BF16),
                        pltpu.VMEM((2, bq, LANES), F32),
                        pltpu.VMEM((2, bq, LANES), F32),
                        pltpu.VMEM((2, bq, LANES), F32),
                        pltpu.VMEM((2, bq, LANES), F32),
                        pltpu.VMEM((2, bq, LANES), F32),
                        pltpu.VMEM((8, bk), F32),
                        pltpu.VMEM((8, bk), F32),
                        pltpu.VMEM((2, bq, 2 * DA_V_DIM), F32)],
    )
    return pl.pallas_call(
        functools.partial(_attn_kernel, bq=bq, bk=bk, nk=nk, seq_len=seq_len, lam_init=lam_init),
        grid_spec=grid_spec,
        out_shape=jax.ShapeDtypeStruct((Tp, D), BF16),
        compiler_params=_cparams("parallel", "parallel", "arbitrary"),
    )(slopes, qkv, qkv, qkv, lq1.reshape(1, hd), lk1.reshape(1, hd), lq2.reshape(1, hd),
      lk2.reshape(1, hd), subln_g.reshape(1, DA_V_DIM))


def _seg_sum(x, member):
    hi, lo = _split2(x)
    per_head = _dot(hi, member) + _dot(lo, member)
    hi, lo = _split2(per_head)
    return _dot_nt(hi, member) + _dot_nt(lo, member)


def _rw_prep_kernel(x_ref, xp_ref, xn_ref, mu_ref, wrkv_ref, w1_ref, w2_ref, a1_ref, a2_ref,
                    g1_ref, g2_ref, vec_ref, ones_ref,
                    r_ref, v_ref, kk_ref, g_ref, bonus_ref, lw_ref, kd_ref, b_ref,
                    *, bm, Lp, seq_len, lora_w, lora_a):
    i = pl.program_id(0)
    x = x_ref[...]
    row = lax.broadcasted_iota(jnp.int32, (bm, 1), 0)
    pos = (i * bm) % Lp + row
    prev_last = xp_ref[7:8, :]
    next_first = xn_ref[0:1, :]
    x_prev = jnp.where(row == 0, prev_last, pltpu.roll(x, 1, 0))
    x_next = jnp.where(row == bm - 1, next_first, pltpu.roll(x, bm - 1, 0))
    x_prev = jnp.where(pos == 0, 0.0, x_prev)
    x_next = jnp.where(pos == seq_len - 1, 0.0, x_next)
    dxp = x_prev - x
    dxn = x_next - x
    valid = pos < seq_len

    def mix(n):
        return (x + mu_ref[n:n + 1, :] * dxp + mu_ref[6 + n:7 + n, :] * dxn).astype(BF16)

    r = _dot(mix(0), wrkv_ref[0])
    k = _dot(mix(1), wrkv_ref[1])
    v = _dot(mix(2), wrkv_ref[2])

    lane = lax.broadcasted_iota(jnp.int32, (bm, 2 * lora_w), 1)
    th = jnp.tanh(_dot(mix(3), w1_ref[...]))
    lw = [_dot(jnp.where((lane >= n * lora_w) & (lane < (n + 1) * lora_w), th, 0.0).astype(BF16),
               w2_ref[...]) for n in range(2)]
    lane = lax.broadcasted_iota(jnp.int32, (bm, 2 * lora_a), 1)
    ah = _dot(mix(4), a1_ref[...])
    la = [_dot(jnp.where((lane >= n * lora_a) & (lane < (n + 1) * lora_a), ah, 0.0).astype(BF16),
               a2_ref[...]) for n in range(2)]
    g = _dot(jax.nn.sigmoid(_dot(mix(5), g1_ref[...])).astype(BF16), g2_ref[...])

    k_k = vec_ref[4:5, :]
    k_a = vec_ref[5:6, :]
    r_k = vec_ref[6:7, :]
    ones_bd = ones_ref[...]

    kkr = k * k_k
    nrm = jnp.sqrt(_seg_sum(kkr * kkr, ones_bd))
    kk = kkr / jnp.maximum(nrm, 1e-12)
    kk = jnp.where(valid, kk, 0.0)
    vz = jnp.where(valid, v, 0.0)

    kd_sum = jnp.zeros_like(k)
    for n in range(2):
        z = -(vec_ref[n:n + 1, :] + lw[n])
        softplus = jnp.maximum(z, 0.0) + jnp.log(1.0 + jnp.exp(-jnp.abs(z)))
        w_log = -softplus - 0.5
        lw_ref[n] = jnp.where(valid, -jnp.exp(w_log), 0.0)
        a = jax.nn.sigmoid(vec_ref[2 + n:3 + n, :] + la[n])
        kd = k * (1.0 + (a - 1.0) * k_a)
        kd_sum = kd_sum + kd
        kd_ref[n] = jnp.where(valid, kd, 0.0)
        b_ref[n] = kk * a

    bonus = _seg_sum(r * kd_sum * r_k, ones_bd) * v
    r_ref[...] = r
    v_ref[...] = vz
    kk_ref[...] = kk
    g_ref[...] = g
    bonus_ref[...] = bonus


def rwkv_prep(h, mu, w_rkv, w1, w2, a1, a2, g1, g2, w0, a0, k_k, k_a, r_k, Lp, seq_len, bm):
    Tp, D = h.shape
    lora_w = w1.shape[-1]
    lora_a = a1.shape[-1]
    lora_g = g1.shape[-1]
    gpad = -(-lora_g // LANES) * LANES
    mu12 = mu.reshape(12, D)
    w1c = jnp.concatenate([w1[0], w1[1]], axis=1).astype(BF16)
    w2c = jnp.concatenate([w2[0], w2[1]], axis=0).astype(BF16)
    a1c = jnp.concatenate([a1[0], a1[1]], axis=1).astype(BF16)
    a2c = jnp.concatenate([a2[0], a2[1]], axis=0).astype(BF16)
    g1p = jnp.pad(g1, ((0, 0), (0, gpad - lora_g))).astype(BF16)
    g2p = jnp.pad(g2, ((0, gpad - lora_g), (0, 0))).astype(BF16)
    vecs = jnp.concatenate([w0, a0, k_k.reshape(1, D), k_a.reshape(1, D), r_k.reshape(1, D),
                            jnp.zeros((1, D), F32)], axis=0)
    seg = jnp.arange(D) // RW_HEAD_DIM
    ones_bd = (seg[:, None] == jnp.arange(LANES)[None, :]).astype(BF16)
    nb8 = bm // 8
    last8 = Tp // 8 - 1
    const2 = lambda i: (0, 0)
    const3 = lambda i: (0, 0, 0)
    row = pl.BlockSpec((bm, D), lambda i: (i, 0))
    row2 = pl.BlockSpec((2, bm, D), lambda i: (0, i, 0))
    sds = jax.ShapeDtypeStruct((Tp, D), F32)
    sds2 = jax.ShapeDtypeStruct((2, Tp, D), F32)
    return pl.pallas_call(
        functools.partial(_rw_prep_kernel, bm=bm, Lp=Lp, seq_len=seq_len, lora_w=lora_w, lora_a=lora_a),
        grid=(Tp // bm,),
        in_specs=[row,
                  pl.BlockSpec((8, D), lambda i: (jnp.maximum(i * nb8 - 1, 0), 0)),
                  pl.BlockSpec((8, D), lambda i: (jnp.minimum((i + 1) * nb8, last8), 0)),
                  pl.BlockSpec((12, D), const2),
                  pl.BlockSpec((3, D, D), const3),
                  pl.BlockSpec((D, 2 * lora_w), const2),
                  pl.BlockSpec((2 * lora_w, D), const2),
                  pl.BlockSpec((D, 2 * lora_a), const2),
                  pl.BlockSpec((2 * lora_a, D), const2),
                  pl.BlockSpec((D, gpad), const2),
                  pl.BlockSpec((gpad, D), const2),
                  pl.BlockSpec((8, D), const2),
                  pl.BlockSpec((D, LANES), const2)],
        out_specs=[row, row, row, row, row, row2, row2, row2],
        out_shape=[sds, sds, sds, sds, sds, sds2, sds2, sds2],
        compiler_params=_cparams("parallel"),
    )(h, h, h, mu12, w_rkv.astype(BF16), w1c, w2c, a1c, a2c, g1p, g2p, vecs, ones_bd)


def _rw_scan_kernel(r_ref, v_ref, kk_ref, lw_ref, kd_ref, b_ref, y_ref, s_ref, *, C, D):
    d = pl.program_id(1)
    c = pl.program_id(2)
    G_ = RW_GROUP
    hpg = G_ // C
    fwd = d == 0

    @pl.when(c == 0)
    def _():
        s_ref[...] = jnp.zeros_like(s_ref)

    sgn = jnp.where(fwd, 1, -1)
    ri = lax.broadcasted_iota(jnp.int32, (C, C), 0)
    ci = lax.broadcasted_iota(jnp.int32, (C, C), 1)
    tri = jnp.where((ri - ci) * sgn >= 0, 1.0, 0.0).astype(BF16)

    lw = lw_ref[...]
    hi = lw.astype(BF16)
    r1 = lw - hi.astype(F32)
    mid = r1.astype(BF16)
    lo = (r1 - mid.astype(F32)).astype(BF16)
    G = _dot(tri, hi) + _dot(tri, mid) + _dot(tri, lo)
    g_end = jnp.where(fwd, G[C - 1:C, :], G[0:1, :])
    e_pos = jnp.exp(G)
    e_neg = jnp.exp(-G)
    e_exc = jnp.exp(G - lw)
    e_rem = jnp.exp(g_end - G)
    kk = kk_ref[...]
    b = b_ref[...]
    kd = kd_ref[...]
    At = (-kk * e_exc).astype(BF16)
    Bt = (b * e_neg).astype(BF16)
    Kt = (kd * e_neg).astype(BF16)
    Rt = (r_ref[...] * e_pos).astype(BF16)
    Bh = (b * e_rem).astype(BF16)
    Kh = (kd * e_rem).astype(BF16)
    Vb = v_ref[...].astype(BF16)
    gam = jnp.exp(g_end)

    R = lax.broadcasted_iota(jnp.int32, (G_, G_), 0)
    Cc = lax.broadcasted_iota(jnp.int32, (G_, G_), 1)
    blk = (R // C) == (Cc // RW_HEAD_DIM)
    dt = jnp.where((R // C) == (Cc // C), (R % C - Cc % C) * sgn, -1)
    strict = dt > 0
    incl = dt >= 0
    eye = jnp.where(R == Cc, 1.0, 0.0)
    zero_b = jnp.zeros((G_, G_), BF16)

    def expand(x, q):
        xq = x[:, q * G_:(q + 1) * G_]
        return jnp.where(blk, jnp.concatenate([xq] * hpg, axis=0), zero_b)

    groups = range(D // G_)
    Ae = [expand(At, q) for q in groups]
    Be = [expand(Bt, q) for q in groups]
    Ke = [expand(Kt, q) for q in groups]
    Re = [expand(Rt, q) for q in groups]
    Ve = [expand(Vb, q) for q in groups]
    Sb = [s_ref[q].astype(BF16) for q in groups]
    N = [jnp.where(strict, _dot_nt(Ae[q], Be[q]), 0.0) for q in groups]
    P = [eye + N[q] for q in groups]
    Xb = [N[q].astype(BF16) for q in groups]
    Aak = [jnp.where(strict, _dot_nt(Ae[q], Ke[q]), 0.0).astype(BF16) for q in groups]
    rhs = [(_dot_nt(Ae[q], Sb[q]) + _dot(Aak[q], Ve[q])).astype(BF16) for q in groups]
    p = 1
    while 2 * p < C:
        Xb = [_dot(Xb[q], Xb[q]).astype(BF16) for q in groups]
        P = [P[q] + _dot(P[q].astype(BF16), Xb[q]) for q in groups]
        p *= 2
    Ub = [_dot(P[q].astype(BF16), rhs[q]).astype(BF16) for q in groups]
    Arb = [jnp.where(incl, _dot_nt(Re[q], Be[q]), 0.0).astype(BF16) for q in groups]
    Ark = [jnp.where(incl, _dot_nt(Re[q], Ke[q]), 0.0).astype(BF16) for q in groups]
    for q in groups:
        Y = _dot_nt(Re[q], Sb[q]) + _dot(Arb[q], Ub[q]) + _dot(Ark[q], Ve[q])
        yq = Y[0:C, :]
        for hb in range(1, hpg):
            yq = yq + Y[hb * C:(hb + 1) * C, :]
        y_ref[:, q * G_:(q + 1) * G_] = yq
    for q in groups:
        s_ref[q] = (s_ref[q] * gam[:, q * G_:(q + 1) * G_] + _dot_tn(Ub[q], expand(Bh, q))
                    + _dot_tn(Ve[q], expand(Kh, q)))


def rwkv_scan(r, v, kk, lw, kd, b, B, Lp):
    Tp, D = r.shape
    C = RW_CHUNK
    nch = Lp // C

    def rb(bi, d, c):
        return bi * nch + jnp.where(d == 0, c, nch - 1 - c)

    shared = pl.BlockSpec((C, D), lambda bi, d, c: (rb(bi, d, c), 0))
    perdir = pl.BlockSpec((None, C, D), lambda bi, d, c: (d, rb(bi, d, c), 0))
    return pl.pallas_call(
        functools.partial(_rw_scan_kernel, C=C, D=D),
        grid=(B, 2, nch),
        in_specs=[shared, shared, shared, perdir, perdir, perdir],
        out_specs=perdir,
        out_shape=jax.ShapeDtypeStruct((2, Tp, D), F32),
        scratch_shapes=[pltpu.VMEM((D // RW_GROUP, RW_GROUP, RW_GROUP), F32)],
        compiler_params=_cparams("parallel", "parallel", "arbitrary"),
    )(r, v, kk, lw, kd, b)


def _rw_post_kernel(yf_ref, yb_ref, g_ref, bonus_ref, res_ref, vec_ref, avg_ref, wo_ref,
                    o_ref, ob_ref):
    avg = avg_ref[...]
    y = yf_ref[...] + yb_ref[...]
    ym = _seg_sum(y, avg)
    yc = y - ym
    yv = _seg_sum(yc * yc, avg)
    yn = yc * lax.rsqrt(yv + RW_LNX_EPS) * vec_ref[0:1, :] + vec_ref[1:2, :]
    out = ((yn + bonus_ref[...]) * g_ref[...]).astype(BF16)
    z = DEEPNORM_ALPHA * res_ref[...] + _dot(out, wo_ref[...])
    o = _layer_norm_rows(z, vec_ref[2:3, :], vec_ref[3:4, :])
    o_ref[...] = o
    ob_ref[...] = o.astype(BF16)


def rwkv_post(y2, g, bonus, res, lnx_g, lnx_b, w_o, ln_g, ln_b, bm):
    Tp, D = res.shape
    vecs = jnp.concatenate([lnx_g.reshape(1, D), lnx_b.reshape(1, D), ln_g.reshape(1, D),
                            ln_b.reshape(1, D), jnp.zeros((4, D), F32)], axis=0)
    seg = jnp.arange(D) // RW_HEAD_DIM
    avg = ((seg[:, None] == jnp.arange(LANES)[None, :]).astype(F32) * RW_HEAD_DIM ** -0.5).astype(BF16)
    row = pl.BlockSpec((bm, D), lambda i: (i, 0))
    return pl.pallas_call(
        _rw_post_kernel,
        grid=(Tp // bm,),
        in_specs=[pl.BlockSpec((None, bm, D), lambda i: (0, i, 0)),
                  pl.BlockSpec((None, bm, D), lambda i: (1, i, 0)),
                  row, row, row,
                  pl.BlockSpec((8, D), lambda i: (0, 0)),
                  pl.BlockSpec((D, LANES), lambda i: (0, 0)),
                  pl.BlockSpec((D, D), lambda i: (0, 0))],
        out_specs=[row, row],
        out_shape=[jax.ShapeDtypeStruct((Tp, D), F32), jax.ShapeDtypeStruct((Tp, D), BF16)],
        compiler_params=_cparams("parallel"),
    )(y2, y2, g, bonus, res, vecs, avg, w_o.astype(BF16))


def _router_kernel(x_ref, wh_ref, wl_ref, b_ref, grow_ref, dcol_ref, drow_ref, cnt_ref):
    x = x_ref[...]
    xh, xl = _split2(x)
    wh = wh_ref[...]
    logits = _dot(xh, wh) + _dot(xh, wl_ref[...]) + _dot(xl, wh) + b_ref[...]
    lane = lax.broadcasted_iota(jnp.int32, logits.shape, 1)
    mx = jnp.max(logits, axis=-1, keepdims=True)
    ex = jnp.exp(logits - mx)
    probs = ex / jnp.sum(ex, axis=-1, keepdims=True)
    big = jnp.int32(LANES)
    m1 = jnp.max(probs, axis=-1, keepdims=True)
    i1 = jnp.min(jnp.where(probs == m1, lane, big), axis=-1, keepdims=True)
    sel1 = lane == i1
    rest = jnp.where(sel1 | (lane >= N_EXPERTS), -1.0, probs)
    m2 = jnp.max(rest, axis=-1, keepdims=True)
    i2 = jnp.min(jnp.where(rest == m2, lane, big), axis=-1, keepdims=True)
    sel2 = lane == i2
    tot = m1 + m2
    gates = jnp.where(sel1, m1 / tot, 0.0) + jnp.where(sel2, m2 / tot, 0.0)
    grow_ref[...] = gates.T

    tm = gates.shape[0]
    sel = jnp.where(gates > 0.0, 1.0, 0.0).astype(BF16)
    ti = lax.broadcasted_iota(jnp.int32, (tm, tm), 0)
    tj = lax.broadcasted_iota(jnp.int32, (tm, tm), 1)
    slot_col = _dot(jnp.where(tj < ti, 1.0, 0.0).astype(BF16), sel)
    dcol_ref[...] = jnp.where(gates > 0.0, slot_col, -1.0)
    excl = _dot_tn(sel, jnp.where(ti < tj, 1.0, 0.0).astype(BF16))
    incl = _dot_tn(sel, jnp.where(ti <= tj, 1.0, 0.0).astype(BF16))
    drow_ref[...] = jnp.where(incl - excl > 0.5, excl, -1.0)
    cnt = jnp.sum(jnp.where(gates > 0.0, 1.0, 0.0), axis=0, keepdims=True)
    cnt_ref[...] = jnp.broadcast_to(cnt, cnt_ref.shape).astype(jnp.int32)


def moe_router(h, w_router, b_router, tm):
    Tp, D = h.shape
    E = w_router.shape[1]
    nt = Tp // tm
    wp = jnp.pad(w_router, ((0, 0), (0, LANES - E)))
    wh = wp.astype(BF16)
    wl = (wp - wh.astype(F32)).astype(BF16)
    bp = jnp.concatenate([b_router, jnp.full((LANES - E,), NEG_BIG, F32)]).reshape(1, LANES)
    return pl.pallas_call(
        _router_kernel,
        grid=(nt,),
        in_specs=[pl.BlockSpec((tm, D), lambda i: (i, 0)),
                  pl.BlockSpec((D, LANES), lambda i: (0, 0)),
                  pl.BlockSpec((D, LANES), lambda i: (0, 0)),
                  pl.BlockSpec((1, LANES), lambda i: (0, 0))],
        out_specs=[pl.BlockSpec((None, LANES, tm), lambda i: (i, 0, 0)),
                   pl.BlockSpec((tm, LANES), lambda i: (i, 0)),
                   pl.BlockSpec((None, LANES, tm), lambda i: (i, 0, 0)),
                   pl.BlockSpec((None, 8, LANES), lambda i: (i, 0, 0))],
        out_shape=[jax.ShapeDtypeStruct((nt, LANES, tm), F32),
                   jax.ShapeDtypeStruct((Tp, LANES), F32),
                   jax.ShapeDtypeStruct((nt, LANES, tm), F32),
                   jax.ShapeDtypeStruct((nt, 8, LANES), jnp.int32)],
        compiler_params=_cparams("parallel"),
    )(h, wh, wl, bp)


def _moe_kernel(cnt_ref, x_ref, grow_ref, dcol_ref, drow_ref, wg_ref, wu_ref, wd_ref,
                y_ref, xg_ref, gc_ref, yo_ref, *, tm, n_f):
    i = pl.program_id(0)
    e = pl.program_id(1)
    f = pl.program_id(2)
    nb = (cnt_ref[i, e] + MOE_ROWS - 1) // MOE_ROWS
    npair = (nb + 1) // 2
    PAIR = 2 * MOE_ROWS

    def rows_of(blk):
        return pl.ds(pl.multiple_of(blk * MOE_ROWS, MOE_ROWS), MOE_ROWS)

    def pair_rows(pr):
        return pl.ds(pl.multiple_of(pr * PAIR, PAIR), PAIR)

    @pl.when((e == 0) & (f == 0))
    def _():
        y_ref[...] = jnp.zeros_like(y_ref)

    @pl.when(f == 0)
    def _():
        x = x_ref[...]
        slot = drow_ref[pl.ds(e, 1), :]
        gate = grow_ref[pl.ds(e, 1), :]

        def gather(pr, carry):
            want = (pr * PAIR + lax.broadcasted_iota(jnp.int32, (PAIR, tm), 0)).astype(F32)
            hit = slot == want
            xg_ref[pair_rows(pr), :] = _dot(jnp.where(hit, 1.0, 0.0).astype(BF16), x).astype(BF16)
            gsel = jnp.sum(jnp.where(hit, gate, 0.0), axis=-1, keepdims=True)
            gc_ref[pair_rows(pr), :] = jnp.broadcast_to(gsel, (PAIR, LANES))
            yo_ref[pair_rows(pr), :] = jnp.zeros((PAIR, yo_ref.shape[1]), F32)
            return carry
        lax.fori_loop(0, npair, gather, 0)

    def ffn(blk, carry):
        xb = xg_ref[rows_of(blk), :]
        g = _dot(xb, wg_ref[...])
        u = _dot(xb, wu_ref[...])
        act = (g * jax.nn.sigmoid(g) * u * gc_ref[rows_of(blk), 0:1]).astype(BF16)
        yo_ref[rows_of(blk), :] += _dot(act, wd_ref[...])
        return carry
    lax.fori_loop(0, nb, ffn, 0)

    @pl.when(f == n_f - 1)
    def _():
        lane_t = lax.broadcasted_iota(jnp.int32, (tm, LANES), 1)
        slot = jnp.sum(jnp.where(lane_t == e, dcol_ref[...], 0.0), axis=-1, keepdims=True)

        def scatter(pr, carry):
            have = (pr * PAIR + lax.broadcasted_iota(jnp.int32, (tm, PAIR), 1)).astype(F32)
            put = jnp.where(slot == have, 1.0, 0.0).astype(BF16)
            y_ref[...] += _dot(put, yo_ref[pair_rows(pr), :].astype(BF16))
            return carry
        lax.fori_loop(0, npair, scatter, 0)


def moe_experts(xb, grow, dcol, drow, cnt, wg, wu, wd, tm, fc):
    Tp, D = xb.shape
    E, _, Fe = wg.shape
    n_f = Fe // fc
    nt = Tp // tm
    grid_spec = pltpu.PrefetchScalarGridSpec(
        num_scalar_prefetch=1,
        grid=(nt, E, n_f),
        in_specs=[pl.BlockSpec((tm, D), lambda i, e, f, c: (i, 0)),
                  pl.BlockSpec((None, LANES, tm), lambda i, e, f, c: (i, 0, 0)),
                  pl.BlockSpec((tm, LANES), lambda i, e, f, c: (i, 0)),
                  pl.BlockSpec((None, LANES, tm), lambda i, e, f, c: (i, 0, 0)),
                  pl.BlockSpec((None, D, fc), lambda i, e, f, c: (e, 0, f)),
                  pl.BlockSpec((None, D, fc), lambda i, e, f, c: (e, 0, f)),
                  pl.BlockSpec((None, fc, D), lambda i, e, f, c: (e, f, 0))],
        out_specs=pl.BlockSpec((tm, D), lambda i, e, f, c: (i, 0)),
        scratch_shapes=[pltpu.VMEM((tm + 2 * MOE_ROWS, D), BF16),
                        pltpu.VMEM((tm + 2 * MOE_ROWS, LANES), F32),
                        pltpu.VMEM((tm + 2 * MOE_ROWS, D), F32)],
    )
    return pl.pallas_call(
        functools.partial(_moe_kernel, tm=tm, n_f=n_f),
        grid_spec=grid_spec,
        out_shape=jax.ShapeDtypeStruct((Tp, D), F32),
        compiler_params=_cparams("parallel", "arbitrary", "arbitrary"),
    )(cnt, xb, grow, dcol, drow, wg, wu, wd)


def _res_ln_kernel(y_ref, res_ref, g_ref, b_ref, o_ref):
    o_ref[...] = _layer_norm_rows(DEEPNORM_ALPHA * res_ref[...] + y_ref[...], g_ref[...], b_ref[...])


def res_ln(y, res, g, b, bm):
    M, N = y.shape
    row = pl.BlockSpec((bm, N), lambda i: (i, 0))
    vec = pl.BlockSpec((1, N), lambda i: (0, 0))
    return pl.pallas_call(
        _res_ln_kernel,
        grid=(M // bm,),
        in_specs=[row, row, vec, vec],
        out_specs=row,
        out_shape=jax.ShapeDtypeStruct((M, N), F32),
        compiler_params=_cparams("parallel"),
    )(y, res, g.reshape(1, N), b.reshape(1, N))


def _largest_divisor(n, cap, mult):
    best = mult
    for c in range(mult, cap + 1, mult):
        if n % c == 0:
            best = c
    return best


def kernel(x, meta, ln_g, ln_b, attn_w_in, attn_w_o, attn_lam_q1, attn_lam_k1, attn_lam_q2, attn_lam_k2, attn_subln_g, ffn_w_gate, ffn_w_up, ffn_w_down, rw_mu, rw_w_rkv, rw_w0, rw_w1, rw_w2, rw_a0, rw_a1, rw_a2, rw_g1, rw_g2, rw_k_k, rw_k_a, rw_r_k, rw_lnx_g, rw_lnx_b, rw_w_o, moe_w_router, moe_b_router, moe_w_gate, moe_w_up, moe_w_down):
    B, S, D = x.shape
    L = S + N_META
    Lp = -(-L // SEQ_MULTIPLE) * SEQ_MULTIPLE
    Tp = B * Lp
    BM = 640
    BM_RW = 256

    h = jnp.concatenate([jnp.broadcast_to(meta[None].astype(x.dtype), (B, N_META, D)), x,
                         jnp.zeros((B, Lp - L, D), x.dtype)], axis=1).reshape(Tp, D)

    qkv = matmul(h, attn_w_in[0].astype(BF16), BF16, BM, 1024)
    o = diff_attention_core(qkv, attn_lam_q1[0], attn_lam_k1[0], attn_lam_q2[0], attn_lam_k2[0],
                            attn_subln_g[0], B, Lp, L, 0, bq=640, bk=1280)
    h, hb = matmul_res_ln(o, attn_w_o[0].astype(BF16), h, ln_g[0, 0], ln_b[0, 0], BM, D)
    F = ffn_w_gate.shape[-1]
    act = ffn_up(hb, ffn_w_gate[0].astype(BF16), ffn_w_up[0].astype(BF16), BM,
                 _largest_divisor(F, 1792, LANES))
    h, hb = matmul_res_ln(act, ffn_w_down[0].astype(BF16), h, ln_g[0, 1], ln_b[0, 1], BM,
                          _largest_divisor(F, 2048, LANES))

    r, v, kk, g, bonus, lw, kd, b = rwkv_prep(
        h, rw_mu[0], rw_w_rkv[0], rw_w1[0], rw_w2[0], rw_a1[0], rw_a2[0], rw_g1[0], rw_g2[0],
        rw_w0[0], rw_a0[0], rw_k_k[0], rw_k_a[0], rw_r_k[0], Lp, L, BM_RW)
    y2 = rwkv_scan(r, v, kk, lw, kd, b, B, Lp)
    h, hb = rwkv_post(y2, g, bonus, h, rw_lnx_g[0], rw_lnx_b[0], rw_w_o[0], ln_g[1, 0], ln_b[1, 0],
                      BM_RW)

    grow, dcol, drow, cnt = moe_router(h, moe_w_router[0], moe_b_router[0], MOE_TILE)
    Fe = moe_w_gate.shape[-1]
    E = moe_w_gate.shape[1]
    y = moe_experts(hb, grow, dcol, drow, cnt[:, 0, :E], moe_w_gate[0].astype(BF16),
                    moe_w_up[0].astype(BF16), moe_w_down[0].astype(BF16), MOE_TILE,
                    _largest_divisor(Fe, 1792, LANES))
    h = res_ln(y, h, ln_g[1, 1], ln_b[1, 1], BM)

    return h.reshape(B, Lp, D)[:, N_META:L]
```
